```python
import jax, jax.numpy as jnp
from jax import lax
import numpy as np

D_MODEL = 4096
BATCH = 1
SEQ = 16384
DEPTH = 4

CHUNK = 64
PLE_DIM = 256
MIX_WIDTH = D_MODEL
CONV_WIDTH = MIX_WIDTH // 2
CONV_GROUPS = 16
CONV_K = 3
RET_WIDTH = MIX_WIDTH - CONV_WIDTH
RET_HEADS = 8
RET_HEAD_DIM = RET_WIDTH // RET_HEADS
D_FF = -(-8 * D_MODEL // (3 * 256)) * 256
GATE_RANK = 256
ROPE_BASE = 10000.0
EPS = 1e-6
IN_WIDTH = 3 * CONV_WIDTH + 4 * RET_WIDTH
SPLITS = (CONV_WIDTH, 2 * CONV_WIDTH, 3 * CONV_WIDTH,
          3 * CONV_WIDTH + RET_WIDTH, 3 * CONV_WIDTH + 2 * RET_WIDTH,
          3 * CONV_WIDTH + 3 * RET_WIDTH)

kernel_name = "hybrid_shortconv_retention_ple_trunk"


def rmsnorm(x, g):
    xf = x.astype(jnp.float32)
    y = xf * lax.rsqrt(jnp.mean(xf * xf, axis=-1, keepdims=True) + EPS)
    return (y * g.astype(jnp.float32)).astype(x.dtype)


def causal_conv(u, w, bias):
    c = u.shape[-1]
    y = lax.conv_general_dilated(
        u, w[:, None, :].astype(u.dtype), window_strides=(1,),
        padding=[(CONV_K - 1, 0)], dimension_numbers=('NWC', 'WIO', 'NWC'),
        feature_group_count=c)
    return y + bias.astype(u.dtype)


def rope(x, pos):
    half = x.shape[-1] // 2
    inv_freq = ROPE_BASE ** (-jnp.arange(half, dtype=jnp.float32) / half)
    ang = pos[:, None] * inv_freq[None, :]
    cos, sin = jnp.cos(ang), jnp.sin(ang)
    x1, x2 = x[..., :half], x[..., half:]
    return jnp.concatenate([x1 * cos - x2 * sin, x2 * cos + x1 * sin], axis=-1)


def retention(q, k, v):
    b, h, s, dk = q.shape
    dv = v.shape[-1]
    n = s // CHUNK
    log_gamma = jnp.log1p(-jnp.exp2(-5.0 - jnp.arange(h, dtype=jnp.float32)))
    idx = jnp.arange(CHUNK, dtype=jnp.float32)
    dist = jnp.abs(idx[:, None] - idx[None, :])
    intra_decay = jnp.exp(dist[None] * log_gamma[:, None, None])
    q_decay = jnp.exp((idx + 1.0)[None] * log_gamma[:, None])
    k_decay = jnp.exp((CHUNK - 1.0 - idx)[None] * log_gamma[:, None])
    chunk_decay = jnp.exp(CHUNK * log_gamma)

    qc = q.reshape(b, h, n, CHUNK, dk)
    kc = k.reshape(b, h, n, CHUNK, dk)
    vc = v.reshape(b, h, n, CHUNK, dv)

    scores = jnp.einsum('bhncd,bhnmd->bhncm', qc, kc) * intra_decay[None, :, None]
    o_intra = jnp.einsum('bhncm,bhnme->bhnce', scores, vc)

    q_in = jnp.moveaxis(qc * q_decay[None, :, None, :, None], 2, 0)
    k_in = jnp.moveaxis(kc * k_decay[None, :, None, :, None], 2, 0)
    v_in = jnp.moveaxis(vc, 2, 0)

    def step(state, xs):
        qn, kn, vn = xs
        out = jnp.einsum('bhcd,bhde->bhce', qn, state)
        state = state * chunk_decay[None, :, None, None] + jnp.einsum('bhcd,bhce->bhde', kn, vn)
        return state, out

    state0 = jnp.zeros((b, h, dk, dv), jnp.float32)
    _, o_inter = lax.scan(step, state0, (q_in, k_in, v_in))
    o = o_intra + jnp.moveaxis(o_inter, 0, 2)
    return o.reshape(b, h, s, dv)


def setup_inputs(seed: int = 0) -> dict:
    key = jax.random.key(seed)
    ks = jax.random.split(key, 20)
    nrm = jax.random.normal

    def gain(k, shape):
        return 1.0 + 0.02 * nrm(k, shape, jnp.float32)

    return {
        "x": nrm(ks[0], (BATCH, SEQ, D_MODEL), jnp.float32),
        "p": nrm(ks[1], (DEPTH, BATCH, SEQ, PLE_DIM), jnp.float32),
        "g_mix": gain(ks[2], (DEPTH, D_MODEL)),
        "w_in": nrm(ks[3], (DEPTH, D_MODEL, IN_WIDTH), jnp.float32) * D_MODEL ** -0.5,
        "conv_w": nrm(ks[4], (DEPTH, CONV_K, CONV_WIDTH), jnp.float32) * CONV_K ** -0.5,
        "conv_b": 0.02 * nrm(ks[5], (DEPTH, CONV_WIDTH), jnp.float32),
        "g_conv_out": gain(ks[6], (DEPTH, CONV_WIDTH)),
        "g_ret_out": gain(ks[7], (DEPTH, RET_WIDTH)),
        "w_out": nrm(ks[8], (DEPTH, MIX_WIDTH, D_MODEL), jnp.float32) * MIX_WIDTH ** -0.5,
        "g_ffn": gain(ks[9], (DEPTH, D_MODEL)),
        "w_ffn_gate": nrm(ks[10], (DEPTH, D_MODEL, D_FF), jnp.float32) * D_MODEL ** -0.5,
        "w_ffn_up": nrm(ks[11], (DEPTH, D_MODEL, D_FF), jnp.float32) * D_MODEL ** -0.5,
        "w_ffn_down": nrm(ks[12], (DEPTH, D_FF, D_MODEL), jnp.float32) * D_FF ** -0.5,
        "g_ple": gain(ks[13], (DEPTH, D_MODEL)),
        "w_ple_gate_down": nrm(ks[14], (DEPTH, D_MODEL, GATE_RANK), jnp.float32) * D_MODEL ** -0.5,
        "w_ple_gate_up": nrm(ks[15], (DEPTH, GATE_RANK, D_MODEL), jnp.float32) * GATE_RANK ** -0.5,
        "w_ple_proj": nrm(ks[16], (DEPTH, PLE_DIM, D_MODEL), jnp.float32) * PLE_DIM ** -0.5,
        "g_final": gain(ks[17], (D_MODEL,)),
    }


def reference(x, p, g_mix, w_in, conv_w, conv_b, g_conv_out, g_ret_out, w_out,
              g_ffn, w_ffn_gate, w_ffn_up, w_ffn_down, g_ple, w_ple_gate_down,
              w_ple_gate_up, w_ple_proj, g_final):
    b, s, _ = x.shape
    pos = jnp.arange(s, dtype=jnp.float32)
    h = x

    def heads(t):
        return t.reshape(b, s, RET_HEADS, RET_HEAD_DIM).transpose(0, 2, 1, 3).astype(jnp.float32)

    for i in range(DEPTH):
        a = rmsnorm(h, g_mix[i])
        proj = a @ w_in[i]
        cb, cc, ch, q, k, v, g = jnp.split(proj, SPLITS, axis=-1)

        y_conv = cb * causal_conv(cc * ch, conv_w[i], conv_b[i])
        y_conv = rmsnorm(y_conv, g_conv_out[i])

        qh = rope(heads(q), pos) * (RET_HEAD_DIM ** -0.5)
        kh = rope(heads(k), pos)
        o = retention(qh, kh, heads(v))
        o = o * lax.rsqrt(jnp.mean(o * o, axis=-1, keepdims=True) + EPS)
        o = o.transpose(0, 2, 1, 3).reshape(b, s, RET_WIDTH) * g_ret_out[i].astype(jnp.float32)
        y_ret = o.astype(h.dtype) * jax.nn.silu(g)

        mix = jnp.concatenate([y_conv, y_ret], axis=-1) @ w_out[i]
        h = h + mix

        f = rmsnorm(h, g_ffn[i])
        h = h + (jax.nn.silu(f @ w_ffn_gate[i]) * (f @ w_ffn_up[i])) @ w_ffn_down[i]

        r = rmsnorm(h, g_ple[i])
        gate = jax.nn.sigmoid((r @ w_ple_gate_down[i]) @ w_ple_gate_up[i])
        h = h + gate * (p[i] @ w_ple_proj[i])

    return rmsnorm(h, g_final)
```

```python
import functools

import jax
import jax.numpy as jnp
from jax import lax
from jax.experimental import pallas as pl
from jax.experimental.pallas import tpu as pltpu

EPS = 1e-6
CHUNK = 64
CONV_WIDTH = 2048
RET_WIDTH = 2048
RET_HEADS = 8
HEAD_DIM = 256
ROPE_BASE = 10000.0
RET_BLOCK = 256

F32 = jnp.float32
BF16 = jnp.bfloat16

V7X_VMEM_BYTES = 64 * 1024 * 1024
VMEM_LIMIT = 56 * 1024 * 1024


def _params(*sem):
    return pltpu.CompilerParams(dimension_semantics=sem, vmem_limit_bytes=VMEM_LIMIT)


def _rmsnorm_kernel(x_ref, g_ref, o_ref):
    x = x_ref[...]
    ms = jnp.mean(x * x, axis=-1, keepdims=True)
    o_ref[...] = (x * lax.rsqrt(ms + EPS) * g_ref[...]).astype(o_ref.dtype)


def rmsnorm(x, g, out_dtype, bm=256):
    s, d = x.shape
    return pl.pallas_call(
        _rmsnorm_kernel,
        grid=(s // bm,),
        in_specs=[pl.BlockSpec((bm, d), lambda i: (i, 0)),
                  pl.BlockSpec((1, d), lambda i: (0, 0))],
        out_specs=pl.BlockSpec((bm, d), lambda i: (i, 0)),
        out_shape=jax.ShapeDtypeStruct((s, d), out_dtype),
        compiler_params=_params("arbitrary"),
        name="rmsnorm",
    )(x, g.reshape(1, d))


def _mm_kernel(a_ref, w_ref, o_ref):
    o_ref[...] = jnp.dot(a_ref[...], w_ref[...],
                         preferred_element_type=F32).astype(o_ref.dtype)


def matmul(a, w, out_dtype, bm, bn, name):
    m, k = a.shape
    _, n = w.shape
    return pl.pallas_call(
        _mm_kernel,
        grid=(m // bm, n // bn),
        in_specs=[pl.BlockSpec((bm, k), lambda i, j: (i, 0)),
                  pl.BlockSpec((k, bn), lambda i, j: (0, j))],
        out_specs=pl.BlockSpec((bm, bn), lambda i, j: (i, j)),
        out_shape=jax.ShapeDtypeStruct((m, n), out_dtype),
        compiler_params=_params("arbitrary", "arbitrary"),
        name=name,
    )(a, w)


def _mm_res_kernel(a_ref, w_ref, r_ref, o_ref):
    o_ref[...] = r_ref[...] + jnp.dot(a_ref[...], w_ref[...],
                                      preferred_element_type=F32)


def matmul_residual(a, w, res, bm, bn, name):
    m, k = a.shape
    _, n = w.shape
    return pl.pallas_call(
        _mm_res_kernel,
        grid=(m // bm, n // bn),
        in_specs=[pl.BlockSpec((bm, k), lambda i, j: (i, 0)),
                  pl.BlockSpec((k, bn), lambda i, j: (0, j)),
                  pl.BlockSpec((bm, bn), lambda i, j: (i, j))],
        out_specs=pl.BlockSpec((bm, bn), lambda i, j: (i, j)),
        out_shape=jax.ShapeDtypeStruct((m, n), F32),
        compiler_params=_params("arbitrary", "arbitrary"),
        name=name,
    )(a, w, res)


def _swiglu_kernel(a_ref, wg_ref, wu_ref, o_ref):
    a = a_ref[...]
    g = jnp.dot(a, wg_ref[...], preferred_element_type=F32)
    u = jnp.dot(a, wu_ref[...], preferred_element_type=F32)
    o_ref[...] = (g * jax.nn.sigmoid(g) * u).astype(o_ref.dtype)


def swiglu_up(a, wg, wu, bm, bn):
    m, k = a.shape
    _, n = wg.shape
    return pl.pallas_call(
        _swiglu_kernel,
        grid=(m // bm, n // bn),
        in_specs=[pl.BlockSpec((bm, k), lambda i, j: (i, 0)),
                  pl.BlockSpec((k, bn), lambda i, j: (0, j)),
                  pl.BlockSpec((k, bn), lambda i, j: (0, j))],
        out_specs=pl.BlockSpec((bm, bn), lambda i, j: (i, j)),
        out_shape=jax.ShapeDtypeStruct((m, n), BF16),
        compiler_params=_params("arbitrary", "arbitrary"),
        name="swiglu_up",
    )(a, wg, wu)


def _rope(x, cos, sin):
    half = HEAD_DIM // 2
    x1, x2 = x[:, :half], x[:, half:]
    return jnp.concatenate([x1 * cos - x2 * sin, x2 * cos + x1 * sin], axis=-1)


def _retention_kernel(lg_ref, q_ref, k_ref, v_ref, g_ref, cos_ref, sin_ref, gro_ref,
                      o_ref, state_ref, dmask_ref, qd_ref, kd_ref):
    t = RET_BLOCK
    lg = lg_ref[0]

    @pl.when(pl.program_id(1) == 0)
    def _():
        state_ref[...] = jnp.zeros_like(state_ref)
        i = lax.broadcasted_iota(jnp.int32, (t, t), 0)
        j = lax.broadcasted_iota(jnp.int32, (t, t), 1)
        dist = jnp.abs(i - j).astype(F32)
        visible = (j // CHUNK) <= (i // CHUNK)
        dmask_ref[...] = jnp.where(visible, jnp.exp(dist * lg), 0.0)
        fi = i.astype(F32)
        qd_ref[...] = jnp.exp((fi + 1.0) * lg)
        kd_ref[...] = jnp.exp((t - 1.0 - fi) * lg)

    cos = cos_ref[...]
    sin = sin_ref[...]
    q = _rope(q_ref[...], cos, sin) * (HEAD_DIM ** -0.5)
    k = _rope(k_ref[...], cos, sin)
    v = v_ref[...].astype(BF16)

    scores = lax.dot_general(q.astype(BF16), k.astype(BF16),
                             (((1,), (1,)), ((), ())), preferred_element_type=F32)
    scores = scores * dmask_ref[...]
    o = jnp.dot(scores.astype(BF16), v, preferred_element_type=F32)
    state = state_ref[...]
    o = o + jnp.dot((q * qd_ref[...]).astype(BF16), state.astype(BF16),
                    preferred_element_type=F32)
    kv = lax.dot_general((k * kd_ref[...]).astype(BF16), v,
                         (((0,), (0,)), ((), ())), preferred_element_type=F32)
    state_ref[...] = state * jnp.exp(float(t) * lg) + kv

    o = o * lax.rsqrt(jnp.mean(o * o, axis=-1, keepdims=True) + EPS)
    g = g_ref[...]
    o_ref[...] = ((o * gro_ref[...]).astype(F32) * (g * jax.nn.sigmoid(g))).astype(o_ref.dtype)


def retention(proj, cos, sin, g_ret_out, log_gamma):
    s = proj.shape[0]
    t = RET_BLOCK
    base = 3 * CONV_WIDTH // HEAD_DIM
    nh = RET_WIDTH // HEAD_DIM

    def col(section):
        return pl.BlockSpec((t, HEAD_DIM), lambda h, i: (i, base + section * nh + h))

    lg = jnp.broadcast_to(log_gamma[:, None, None], (RET_HEADS, 1, HEAD_DIM))
    return pl.pallas_call(
        _retention_kernel,
        grid=(RET_HEADS, s // t),
        in_specs=[pl.BlockSpec((1, 1, HEAD_DIM), lambda h, i: (h, 0, 0)),
                  col(0), col(1), col(2), col(3),
                  pl.BlockSpec((t, HEAD_DIM // 2), lambda h, i: (i, 0)),
                  pl.BlockSpec((t, HEAD_DIM // 2), lambda h, i: (i, 0)),
                  pl.BlockSpec((1, HEAD_DIM), lambda h, i: (0, h))],
        out_specs=pl.BlockSpec((t, HEAD_DIM), lambda h, i: (i, h)),
        out_shape=jax.ShapeDtypeStruct((s, RET_WIDTH), BF16),
        scratch_shapes=[pltpu.VMEM((HEAD_DIM, HEAD_DIM), F32),
                        pltpu.VMEM((t, t), F32),
                        pltpu.VMEM((t, HEAD_DIM), F32),
                        pltpu.VMEM((t, HEAD_DIM), F32)],
        compiler_params=_params("arbitrary", "arbitrary"),
        name="retention",
    )(lg, proj, proj, proj, proj, cos, sin, g_ret_out.reshape(1, RET_WIDTH))


def _ple_kernel(h_ref, p_ref, gple_ref, wd_ref, wu_ref, wp_ref, gnext_ref, *out_refs,
                emit_h):
    h = h_ref[...]
    ms = jnp.mean(h * h, axis=-1, keepdims=True)
    r = (h * lax.rsqrt(ms + EPS) * gple_ref[...]).astype(BF16)
    low = jnp.dot(r, wd_ref[...], preferred_element_type=F32)
    gate = jax.nn.sigmoid(jnp.dot(low.astype(BF16), wu_ref[...], preferred_element_type=F32))
    emb = jnp.dot(p_ref[...].astype(BF16), wp_ref[...], preferred_element_type=F32)
    hn = h + gate * emb
    ms2 = jnp.mean(hn * hn, axis=-1, keepdims=True)
    normed = hn * lax.rsqrt(ms2 + EPS) * gnext_ref[...]
    if emit_h:
        out_refs[0][...] = hn
        out_refs[1][...] = normed.astype(out_refs[1].dtype)
    else:
        out_refs[0][...] = normed.astype(out_refs[0].dtype)


def ple_and_norm(h, p, g_ple, wd, wu, wp, g_next, last, bm=256):
    s, d = h.shape
    pdim = p.shape[1]
    rank = wd.shape[1]
    row = pl.BlockSpec((bm, d), lambda i: (i, 0))
    vec = pl.BlockSpec((1, d), lambda i: (0, 0))
    if last:
        out_specs = row
        out_shape = jax.ShapeDtypeStruct((s, d), F32)
    else:
        out_specs = [row, row]
        out_shape = [jax.ShapeDtypeStruct((s, d), F32), jax.ShapeDtypeStruct((s, d), BF16)]
    return pl.pallas_call(
        functools.partial(_ple_kernel, emit_h=not last),
        grid=(s // bm,),
        in_specs=[row,
                  pl.BlockSpec((bm, pdim), lambda i: (i, 0)),
                  vec,
                  pl.BlockSpec((d, rank), lambda i: (0, 0)),
                  pl.BlockSpec((rank, d), lambda i: (0, 0)),
                  pl.BlockSpec((pdim, d), lambda i: (0, 0)),
                  vec],
        out_specs=out_specs,
        out_shape=out_shape,
        compiler_params=_params("arbitrary"),
        name="ple",
    )(h, p, g_ple.reshape(1, d), wd, wu, wp, g_next.reshape(1, d))


def kernel(x, p, g_mix, w_in, conv_w, conv_b, g_conv_out, g_ret_out, w_out, g_ffn,
           w_ffn_gate, w_ffn_up, w_ffn_down, g_ple, w_ple_gate_down, w_ple_gate_up,
           w_ple_proj, g_final):
    b, s, d = x.shape
    assert b == 1
    depth = w_in.shape[0]
    h = x.reshape(s, d)

    pos = jnp.arange(s, dtype=F32)
    half = HEAD_DIM // 2
    inv_freq = ROPE_BASE ** (-jnp.arange(half, dtype=F32) / half)
    ang = pos[:, None] * inv_freq[None, :]
    cos, sin = jnp.cos(ang), jnp.sin(ang)
    log_gamma = jnp.log1p(-jnp.exp2(-5.0 - jnp.arange(RET_HEADS, dtype=F32)))

    a = rmsnorm(h, g_mix[0], BF16)
    for i in range(depth):
        proj = matmul(a, w_in[i].astype(BF16), F32, 1024, 1024, "in_proj")

        cb = proj[:, :CONV_WIDTH]
        u = proj[:, CONV_WIDTH:2 * CONV_WIDTH] * proj[:, 2 * CONV_WIDTH:3 * CONV_WIDTH]
        u1 = jnp.pad(u, ((1, 0), (0, 0)))[:-1]
        u2 = jnp.pad(u, ((2, 0), (0, 0)))[:-2]
        cw = conv_w[i]
        conv = u2 * cw[0] + u1 * cw[1] + u * cw[2] + conv_b[i]
        yc = cb * conv
        yc = yc * lax.rsqrt(jnp.mean(yc * yc, axis=-1, keepdims=True) + EPS) * g_conv_out[i]

        y_ret = retention(proj, cos, sin, g_ret_out[i], log_gamma)
        y = jnp.concatenate([yc.astype(BF16), y_ret], axis=-1)
        h = matmul_residual(y, w_out[i].astype(BF16), h, 1024, 1024, "out_proj")

        f = rmsnorm(h, g_ffn[i], BF16)
        hid = swiglu_up(f, w_ffn_gate[i].astype(BF16), w_ffn_up[i].astype(BF16), 1024, 256)
        h = matmul_residual(hid, w_ffn_down[i].astype(BF16), h, 512, 256, "ffn_down")

        last = i == depth - 1
        g_next = g_final if last else g_mix[i + 1]
        res = ple_and_norm(h, p[i, 0], g_ple[i], w_ple_gate_down[i].astype(BF16),
                           w_ple_gate_up[i].astype(BF16), w_ple_proj[i].astype(BF16),
                           g_next, last)
        if last:
            return res.reshape(b, s, d)
        h, a = res
```

```python
import functools

import jax
import jax.numpy as jnp
from jax import lax
from jax.experimental import pallas as pl
from jax.experimental.pallas import tpu as pltpu

EPS = 1e-6
CHUNK = 64
CONV_WIDTH = 2048
RET_WIDTH = 2048
RET_HEADS = 8
HEAD_DIM = 256
HALF = HEAD_DIM // 2
ROPE_BASE = 10000.0

F32 = jnp.float32
BF16 = jnp.bfloat16

SUBLANES = 8
V7X_VMEM_LIMIT = 58 * 1024 * 1024

ROW_TILE = 1024
CONV_COLS = 256
HEAD_COLS = 512
OUT_COLS = 512
FFN_ROWS = 2048
FFN_COLS = 256
DOWN_ROWS = 512
DOWN_COLS = 512
RET_ROWS = 1024
RET_BLOCK = 256
NORM_ROWS = 256


def _params(*sem):
    return pltpu.CompilerParams(dimension_semantics=sem, vmem_limit_bytes=V7X_VMEM_LIMIT)


def _silu(x):
    return x * jax.nn.sigmoid(x)


def _rmsnorm_kernel(x_ref, g_ref, o_ref):
    x = x_ref[...]
    ms = jnp.mean(x * x, axis=-1, keepdims=True)
    o_ref[...] = (x * lax.rsqrt(ms + EPS) * g_ref[...]).astype(o_ref.dtype)


def rmsnorm(x, g, out_dtype):
    s, d = x.shape
    bm = NORM_ROWS
    return pl.pallas_call(
        _rmsnorm_kernel,
        grid=(s // bm,),
        in_specs=[pl.BlockSpec((bm, d), lambda i: (i, 0)),
                  pl.BlockSpec((1, d), lambda i: (0, 0))],
        out_specs=pl.BlockSpec((bm, d), lambda i: (i, 0)),
        out_shape=jax.ShapeDtypeStruct((s, d), out_dtype),
        compiler_params=_params("arbitrary"),
        name="rmsnorm",
    )(x, g.reshape(1, d))


def _conv_proj_kernel(a_ref, wb_ref, wc_ref, wh_ref, cw_ref, cbias_ref, o_ref, wbf_ref, halo_ref):
    tc = CONV_COLS
    m = pl.program_id(1)

    @pl.when(m == 0)
    def _():
        wbf_ref[:, 0:tc] = wb_ref[...].astype(BF16)
        wbf_ref[:, tc:2 * tc] = wc_ref[...].astype(BF16)
        wbf_ref[:, 2 * tc:3 * tc] = wh_ref[...].astype(BF16)
        halo_ref[...] = jnp.zeros_like(halo_ref)

    acc = jnp.dot(a_ref[...], wbf_ref[...], preferred_element_type=F32)
    cb = acc[:, 0:tc]
    u = acc[:, tc:2 * tc] * acc[:, 2 * tc:3 * tc]
    bm = u.shape[0]
    row = lax.broadcasted_iota(jnp.int32, (bm, tc), 0)
    halo = halo_ref[...]
    prev1 = halo[SUBLANES - 1:SUBLANES, :]
    prev2 = halo[SUBLANES - 2:SUBLANES - 1, :]
    u1 = jnp.where(row == 0, prev1, pltpu.roll(u, 1, axis=0))
    u2 = jnp.where(row == 0, prev2, jnp.where(row == 1, prev1, pltpu.roll(u, 2, axis=0)))
    halo_ref[...] = u[bm - SUBLANES:, :]
    cw = cw_ref[...]
    conv = u2 * cw[0:1, :] + u1 * cw[1:2, :] + u * cw[2:3, :] + cbias_ref[...]
    o_ref[...] = (cb * conv).astype(o_ref.dtype)


def conv_proj(a, w_in, layer, conv_w, conv_b):
    s, d = a.shape
    bm, tc = ROW_TILE, CONV_COLS
    nblk = CONV_WIDTH // tc

    def wspec(section):
        return pl.BlockSpec((None, d, tc), lambda n, m: (layer, 0, section * nblk + n))

    return pl.pallas_call(
        _conv_proj_kernel,
        grid=(nblk, s // bm),
        in_specs=[pl.BlockSpec((bm, d), lambda n, m: (m, 0)),
                  wspec(0), wspec(1), wspec(2),
                  pl.BlockSpec((3, tc), lambda n, m: (0, n)),
                  pl.BlockSpec((1, tc), lambda n, m: (0, n))],
        out_specs=pl.BlockSpec((bm, tc), lambda n, m: (m, n)),
        out_shape=jax.ShapeDtypeStruct((s, CONV_WIDTH), BF16),
        scratch_shapes=[pltpu.VMEM((d, 3 * tc), BF16),
                        pltpu.VMEM((SUBLANES, tc), F32)],
        compiler_params=_params("arbitrary", "arbitrary"),
        name="conv_proj",
    )(a, w_in, w_in, w_in, conv_w, conv_b.reshape(1, CONV_WIDTH))


def _head_proj_kernel(a_ref, w_ref, cos_ref, sin_ref, o_ref, wbf_ref, *, blocks_per_section):
    n = pl.program_id(0)
    m = pl.program_id(1)
    nb = blocks_per_section

    @pl.when(m == 0)
    def _():
        wbf_ref[...] = w_ref[...].astype(BF16)

    def project():
        return jnp.dot(a_ref[...], wbf_ref[...], preferred_element_type=F32)

    @pl.when(n < 2 * nb)
    def _():
        acc = project()
        scale = jnp.where(n < nb, HEAD_DIM ** -0.5, 1.0).astype(F32)
        cos = cos_ref[...]
        sin = sin_ref[...]
        for hd in range(HEAD_COLS // HEAD_DIM):
            c = hd * HEAD_DIM
            x1 = acc[:, c:c + HALF]
            x2 = acc[:, c + HALF:c + HEAD_DIM]
            o_ref[:, c:c + HALF] = ((x1 * cos - x2 * sin) * scale).astype(o_ref.dtype)
            o_ref[:, c + HALF:c + HEAD_DIM] = ((x2 * cos + x1 * sin) * scale).astype(o_ref.dtype)

    @pl.when(jnp.logical_and(n >= 2 * nb, n < 3 * nb))
    def _():
        o_ref[...] = project().astype(o_ref.dtype)

    @pl.when(n >= 3 * nb)
    def _():
        o_ref[...] = _silu(project()).astype(o_ref.dtype)


def head_proj(a, w_in, layer, cos, sin):
    s, d = a.shape
    bm, bn = ROW_TILE, HEAD_COLS
    first = 3 * CONV_WIDTH // bn
    nb = RET_WIDTH // bn
    return pl.pallas_call(
        functools.partial(_head_proj_kernel, blocks_per_section=nb),
        grid=(4 * nb, s // bm),
        in_specs=[pl.BlockSpec((bm, d), lambda n, m: (m, 0)),
                  pl.BlockSpec((None, d, bn), lambda n, m: (layer, 0, first + n)),
                  pl.BlockSpec((bm, HALF), lambda n, m: (m, 0)),
                  pl.BlockSpec((bm, HALF), lambda n, m: (m, 0))],
        out_specs=pl.BlockSpec((bm, bn), lambda n, m: (m, n)),
        out_shape=jax.ShapeDtypeStruct((s, 4 * RET_WIDTH), BF16),
        scratch_shapes=[pltpu.VMEM((d, bn), BF16)],
        compiler_params=_params("arbitrary", "arbitrary"),
        name="head_proj",
    )(a, w_in, cos, sin)


def _retention_kernel(lg_ref, q_ref, k_ref, v_ref, sg_ref, gro_ref, o_ref,
                      state_ref, dmask_ref, qd_ref, kd_ref):
    t = RET_BLOCK
    lg = lg_ref[0]

    @pl.when(pl.program_id(1) == 0)
    def _():
        state_ref[...] = jnp.zeros_like(state_ref)
        i = lax.broadcasted_iota(jnp.int32, (t, t), 0)
        j = lax.broadcasted_iota(jnp.int32, (t, t), 1)
        dist = jnp.abs(i - j).astype(F32)
        visible = (j // CHUNK) <= (i // CHUNK)
        dmask_ref[...] = jnp.where(visible, jnp.exp(dist * lg), 0.0)
        fi = i.astype(F32)
        qd_ref[...] = jnp.exp((fi + 1.0) * lg)
        kd_ref[...] = jnp.exp((t - 1.0 - fi) * lg)

    block_decay = jnp.exp(float(t) * lg)
    gro = gro_ref[...]
    for blk in range(RET_ROWS // t):
        rows = pl.ds(blk * t, t)
        q = q_ref[rows, :]
        k = k_ref[rows, :]
        v = v_ref[rows, :]
        scores = lax.dot_general(q, k, (((1,), (1,)), ((), ())), preferred_element_type=F32)
        scores = scores * dmask_ref[...]
        o = jnp.dot(scores.astype(BF16), v, preferred_element_type=F32)
        state = state_ref[...]
        qs = (q.astype(F32) * qd_ref[...]).astype(BF16)
        o = o + jnp.dot(qs, state.astype(BF16), preferred_element_type=F32)
        ks = (k.astype(F32) * kd_ref[...]).astype(BF16)
        kv = lax.dot_general(ks, v, (((0,), (0,)), ((), ())), preferred_element_type=F32)
        state_ref[...] = state * block_decay + kv
        o = o * lax.rsqrt(jnp.mean(o * o, axis=-1, keepdims=True) + EPS)
        o_ref[rows, :] = ((o * gro) * sg_ref[rows, :].astype(F32)).astype(o_ref.dtype)


def retention(qkvg, g_ret_out, log_gamma):
    s = qkvg.shape[0]
    bm = RET_ROWS
    nh = RET_HEADS

    def col(section):
        return pl.BlockSpec((bm, HEAD_DIM), lambda h, i: (i, section * nh + h))

    lg = jnp.broadcast_to(log_gamma[:, None, None], (nh, 1, HEAD_DIM))
    return pl.pallas_call(
        _retention_kernel,
        grid=(nh, s // bm),
        in_specs=[pl.BlockSpec((1, 1, HEAD_DIM), lambda h, i: (h, 0, 0)),
                  col(0), col(1), col(2), col(3),
                  pl.BlockSpec((1, HEAD_DIM), lambda h, i: (0, h))],
        out_specs=pl.BlockSpec((bm, HEAD_DIM), lambda h, i: (i, h)),
        out_shape=jax.ShapeDtypeStruct((s, RET_WIDTH), BF16),
        scratch_shapes=[pltpu.VMEM((HEAD_DIM, HEAD_DIM), F32),
                        pltpu.VMEM((RET_BLOCK, RET_BLOCK), F32),
                        pltpu.VMEM((RET_BLOCK, HEAD_DIM), F32),
                        pltpu.VMEM((RET_BLOCK, HEAD_DIM), F32)],
        compiler_params=_params("arbitrary", "arbitrary"),
        name="retention",
    )(lg, qkvg, qkvg, qkvg, qkvg, g_ret_out.reshape(1, RET_WIDTH))


def _out_proj_kernel(yc_ref, yr_ref, wt_ref, wb_ref, gc_ref, h_ref, o_ref, wtbf_ref, wbbf_ref):
    @pl.when(pl.program_id(1) == 0)
    def _():
        wtbf_ref[...] = wt_ref[...].astype(BF16)
        wbbf_ref[...] = wb_ref[...].astype(BF16)

    yc = yc_ref[...].astype(F32)
    ms = jnp.mean(yc * yc, axis=-1, keepdims=True)
    ycn = (yc * lax.rsqrt(ms + EPS) * gc_ref[...]).astype(BF16)
    acc = jnp.dot(ycn, wtbf_ref[...], preferred_element_type=F32)
    acc = acc + jnp.dot(yr_ref[...], wbbf_ref[...], preferred_element_type=F32)
    o_ref[...] = h_ref[...] + acc


def out_proj(yc, yr, w_out, layer, g_conv_out, h):
    s, d = h.shape
    bm, bn = ROW_TILE, OUT_COLS
    kc, kr = yc.shape[1], yr.shape[1]
    assert kc == kr
    return pl.pallas_call(
        _out_proj_kernel,
        grid=(d // bn, s // bm),
        in_specs=[pl.BlockSpec((bm, kc), lambda n, m: (m, 0)),
                  pl.BlockSpec((bm, kr), lambda n, m: (m, 0)),
                  pl.BlockSpec((None, kc, bn), lambda n, m: (layer, 0, n)),
                  pl.BlockSpec((None, kr, bn), lambda n, m: (layer, 1, n)),
                  pl.BlockSpec((1, kc), lambda n, m: (0, 0)),
                  pl.BlockSpec((bm, bn), lambda n, m: (m, n))],
        out_specs=pl.BlockSpec((bm, bn), lambda n, m: (m, n)),
        out_shape=jax.ShapeDtypeStruct((s, d), F32),
        scratch_shapes=[pltpu.VMEM((kc, bn), BF16), pltpu.VMEM((kr, bn), BF16)],
        compiler_params=_params("arbitrary", "arbitrary"),
        name="out_proj",
    )(yc, yr, w_out, w_out, g_conv_out.reshape(1, kc), h)


def _swiglu_kernel(a_ref, wg_ref, wu_ref, o_ref, wbf_ref):
    tf = FFN_COLS

    @pl.when(pl.program_id(1) == 0)
    def _():
        wbf_ref[:, 0:tf] = wg_ref[...].astype(BF16)
        wbf_ref[:, tf:2 * tf] = wu_ref[...].astype(BF16)

    acc = jnp.dot(a_ref[...], wbf_ref[...], preferred_element_type=F32)
    o_ref[...] = (_silu(acc[:, 0:tf]) * acc[:, tf:2 * tf]).astype(o_ref.dtype)


def swiglu_up(a, wg, wu, layer):
    s, d = a.shape
    n = wg.shape[2]
    bm, tf = FFN_ROWS, FFN_COLS
    return pl.pallas_call(
        _swiglu_kernel,
        grid=(n // tf, s // bm),
        in_specs=[pl.BlockSpec((bm, d), lambda j, m: (m, 0)),
                  pl.BlockSpec((None, d, tf), lambda j, m: (layer, 0, j)),
                  pl.BlockSpec((None, d, tf), lambda j, m: (layer, 0, j))],
        out_specs=pl.BlockSpec((bm, tf), lambda j, m: (m, j)),
        out_shape=jax.ShapeDtypeStruct((s, n), BF16),
        scratch_shapes=[pltpu.VMEM((d, 2 * tf), BF16)],
        compiler_params=_params("arbitrary", "arbitrary"),
        name="swiglu_up",
    )(a, wg, wu)


def _down_kernel(a_ref, w_ref, h_ref, o_ref):
    o_ref[...] = h_ref[...] + jnp.dot(a_ref[...], w_ref[...], preferred_element_type=F32)


def ffn_down(hid, w, h):
    s, k = hid.shape
    d = w.shape[1]
    bm, bn = DOWN_ROWS, DOWN_COLS
    return pl.pallas_call(
        _down_kernel,
        grid=(s // bm, d // bn),
        in_specs=[pl.BlockSpec((bm, k), lambda i, j: (i, 0)),
                  pl.BlockSpec((k, bn), lambda i, j: (0, j)),
                  pl.BlockSpec((bm, bn), lambda i, j: (i, j))],
        out_specs=pl.BlockSpec((bm, bn), lambda i, j: (i, j)),
        out_shape=jax.ShapeDtypeStruct((s, d), F32),
        compiler_params=_params("arbitrary", "arbitrary"),
        name="ffn_down",
    )(hid, w, h)


def _ple_kernel(h_ref, p_ref, gple_ref, wd_ref, wu_ref, wp_ref, gnext_ref, *out_refs, emit_h):
    h = h_ref[...]
    ms = jnp.mean(h * h, axis=-1, keepdims=True)
    r = (h * lax.rsqrt(ms + EPS) * gple_ref[...]).astype(BF16)
    low = jnp.dot(r, wd_ref[...], preferred_element_type=F32)
    gate = jax.nn.sigmoid(jnp.dot(low.astype(BF16), wu_ref[...], preferred_element_type=F32))
    emb = jnp.dot(p_ref[...].astype(BF16), wp_ref[...], preferred_element_type=F32)
    hn = h + gate * emb
    ms2 = jnp.mean(hn * hn, axis=-1, keepdims=True)
    normed = hn * lax.rsqrt(ms2 + EPS) * gnext_ref[...]
    if emit_h:
        out_refs[0][...] = hn
        out_refs[1][...] = normed.astype(out_refs[1].dtype)
    else:
        out_refs[0][...] = normed.astype(out_refs[0].dtype)


def ple_and_norm(h, p, layer, g_ple, wd, wu, wp, g_next, last):
    s, d = h.shape
    bm = NORM_ROWS
    pdim = p.shape[-1]
    rank = wd.shape[1]
    row = pl.BlockSpec((bm, d), lambda i: (i, 0))
    vec = pl.BlockSpec((1, d), lambda i: (0, 0))
    if last:
        out_specs = row
        out_shape = jax.ShapeDtypeStruct((s, d), F32)
    else:
        out_specs = [row, row]
        out_shape = [jax.ShapeDtypeStruct((s, d), F32), jax.ShapeDtypeStruct((s, d), BF16)]
    return pl.pallas_call(
        functools.partial(_ple_kernel, emit_h=not last),
        grid=(s // bm,),
        in_specs=[row,
                  pl.BlockSpec((None, None, bm, pdim), lambda i: (layer, 0, i, 0)),
                  vec,
                  pl.BlockSpec((d, rank), lambda i: (0, 0)),
                  pl.BlockSpec((rank, d), lambda i: (0, 0)),
                  pl.BlockSpec((pdim, d), lambda i: (0, 0)),
                  vec],
        out_specs=out_specs,
        out_shape=out_shape,
        compiler_params=_params("arbitrary"),
        name="ple",
    )(h, p, g_ple.reshape(1, d), wd, wu, wp, g_next.reshape(1, d))


def kernel(x, p, g_mix, w_in, conv_w, conv_b, g_conv_out, g_ret_out, w_out, g_ffn,
           w_ffn_gate, w_ffn_up, w_ffn_down, g_ple, w_ple_gate_down, w_ple_gate_up,
           w_ple_proj, g_final):
    b, s, d = x.shape
    assert b == 1
    depth = w_in.shape[0]
    h = x.reshape(s, d)

    pos = jnp.arange(s, dtype=F32)
    inv_freq = ROPE_BASE ** (-jnp.arange(HALF, dtype=F32) / HALF)
    ang = pos[:, None] * inv_freq[None, :]
    cos, sin = jnp.cos(ang), jnp.sin(ang)
    log_gamma = jnp.log1p(-jnp.exp2(-5.0 - jnp.arange(RET_HEADS, dtype=F32)))

    a = rmsnorm(h, g_mix[0], BF16)
    for i in range(depth):
        yc = conv_proj(a, w_in, i, conv_w[i], conv_b[i])
        qkvg = head_proj(a, w_in, i, cos, sin)
        yr = retention(qkvg, g_ret_out[i], log_gamma)
        h = out_proj(yc, yr, w_out, i, g_conv_out[i], h)

        f = rmsnorm(h, g_ffn[i], BF16)
        hid = swiglu_up(f, w_ffn_gate, w_ffn_up, i)
        h = ffn_down(hid, w_ffn_down[i].astype(BF16), h)

        last = i == depth - 1
        g_next = g_final if last else g_mix[i + 1]
        res = ple_and_norm(h, p, i, g_ple[i], w_ple_gate_down[i].astype(BF16),
                           w_ple_gate_up[i].astype(BF16), w_ple_proj[i].astype(BF16),
                           g_next, last)
        if last:
            return res.reshape(b, s, d)
        h, a = res
```

```python
import functools

import jax
import jax.numpy as jnp
from jax import lax
from jax.experimental import pallas as pl
from jax.experimental.pallas import tpu as pltpu

EPS = 1e-6
CHUNK = 64
CONV_WIDTH = 2048
RET_WIDTH = 2048
RET_HEADS = 8
HEAD_DIM = 256
HALF = HEAD_DIM // 2
ROPE_BASE = 10000.0

F32 = jnp.float32
BF16 = jnp.bfloat16

SUBLANES = 8
LANES = 128
V7X_VMEM_LIMIT = 62 * 1024 * 1024

CONV_ROWS = 1024
CONV_COLS = 256
HEAD_ROWS = 1024
HEAD_COLS = 1024
OUT_ROWS = 512
OUT_COLS = 1024
FFN_ROWS = 2048
FFN_COLS = 256
DOWN_ROWS = 512
DOWN_COLS = 512
RET_ROWS = 2048
RET_BLOCK = 256
NORM_ROWS = 256


def _params(*sem):
    return pltpu.CompilerParams(dimension_semantics=sem, vmem_limit_bytes=V7X_VMEM_LIMIT)


def _silu(x):
    return x * jax.nn.sigmoid(x)


def _rmsnorm_kernel(x_ref, g_ref, o_ref):
    x = x_ref[...]
    ms = jnp.mean(x * x, axis=-1, keepdims=True)
    o_ref[...] = (x * lax.rsqrt(ms + EPS) * g_ref[...]).astype(o_ref.dtype)


def rmsnorm(x, g, out_dtype):
    s, d = x.shape
    bm = NORM_ROWS
    return pl.pallas_call(
        _rmsnorm_kernel,
        grid=(s // bm,),
        in_specs=[pl.BlockSpec((bm, d), lambda i: (i, 0)),
                  pl.BlockSpec((1, d), lambda i: (0, 0))],
        out_specs=pl.BlockSpec((bm, d), lambda i: (i, 0)),
        out_shape=jax.ShapeDtypeStruct((s, d), out_dtype),
        compiler_params=_params("arbitrary"),
        name="rmsnorm",
    )(x, g.reshape(1, d))


def _prefetched_weight_tile(w_hbm, windows, stage_ref, wbf_ref, sems):
    n = pl.program_id(0)

    def copies(j):
        return [pltpu.make_async_copy(src, dst, sems.at[i])
                for i, (src, dst) in enumerate(windows(j))]

    @pl.when(n == 0)
    def _():
        for c in copies(0):
            c.start()

    for c in copies(n):
        c.wait()
    wbf_ref[...] = stage_ref[...].astype(BF16)

    @pl.when(n + 1 < pl.num_programs(0))
    def _():
        for c in copies(n + 1):
            c.start()


def _conv_proj_kernel(a_ref, w_hbm, cw_ref, cbias_ref, o_ref, stage_ref, wbf_ref, halo_ref, sems,
                      *, layer):
    tc = CONV_COLS

    def windows(j):
        return [(w_hbm.at[layer, :, pl.ds(sec * CONV_WIDTH + j * tc, tc)],
                 stage_ref.at[:, pl.ds(sec * tc, tc)]) for sec in range(3)]

    @pl.when(pl.program_id(1) == 0)
    def _():
        _prefetched_weight_tile(w_hbm, windows, stage_ref, wbf_ref, sems)
        halo_ref[...] = jnp.zeros_like(halo_ref)

    acc = jnp.dot(a_ref[...], wbf_ref[...], preferred_element_type=F32)
    cb = acc[:, 0:tc]
    u = acc[:, tc:2 * tc] * acc[:, 2 * tc:3 * tc]
    bm = u.shape[0]
    row = lax.broadcasted_iota(jnp.int32, (bm, tc), 0)
    halo = halo_ref[...]
    prev1 = halo[SUBLANES - 1:SUBLANES, :]
    prev2 = halo[SUBLANES - 2:SUBLANES - 1, :]
    u1 = jnp.where(row == 0, prev1, pltpu.roll(u, 1, axis=0))
    u2 = jnp.where(row == 0, prev2, jnp.where(row == 1, prev1, pltpu.roll(u, 2, axis=0)))
    halo_ref[...] = u[bm - SUBLANES:, :]
    cw = cw_ref[...]
    conv = u2 * cw[0:1, :] + u1 * cw[1:2, :] + u * cw[2:3, :] + cbias_ref[...]
    o_ref[...] = (cb * conv).astype(o_ref.dtype)


def conv_proj(a, w_in, layer, conv_w, conv_b):
    s, d = a.shape
    bm, tc = CONV_ROWS, CONV_COLS
    return pl.pallas_call(
        functools.partial(_conv_proj_kernel, layer=layer),
        grid=(CONV_WIDTH // tc, s // bm),
        in_specs=[pl.BlockSpec((bm, d), lambda n, m: (m, 0)),
                  pl.BlockSpec(memory_space=pl.ANY),
                  pl.BlockSpec((3, tc), lambda n, m: (0, n)),
                  pl.BlockSpec((1, tc), lambda n, m: (0, n))],
        out_specs=pl.BlockSpec((bm, tc), lambda n, m: (m, n)),
        out_shape=jax.ShapeDtypeStruct((s, CONV_WIDTH), BF16),
        scratch_shapes=[pltpu.VMEM((d, 3 * tc), F32),
                        pltpu.VMEM((d, 3 * tc), BF16),
                        pltpu.VMEM((SUBLANES, tc), F32),
                        pltpu.SemaphoreType.DMA((3,))],
        compiler_params=_params("arbitrary", "arbitrary"),
        name="conv_proj",
    )(a, w_in, conv_w, conv_b.reshape(1, CONV_WIDTH))


def _head_proj_kernel(a_ref, w_hbm, cos_ref, sin_ref, o_ref, stage_ref, wbf_ref, sems,
                      *, layer, blocks_per_section):
    n = pl.program_id(0)
    nb = blocks_per_section
    bn = HEAD_COLS

    def windows(j):
        return [(w_hbm.at[layer, :, pl.ds(3 * CONV_WIDTH + j * bn, bn)], stage_ref)]

    @pl.when(pl.program_id(1) == 0)
    def _():
        _prefetched_weight_tile(w_hbm, windows, stage_ref, wbf_ref, sems)

    def project():
        return jnp.dot(a_ref[...], wbf_ref[...], preferred_element_type=F32)

    @pl.when(n < 2 * nb)
    def _():
        acc = project()
        scale = jnp.where(n < nb, HEAD_DIM ** -0.5, 1.0).astype(F32)
        cos = cos_ref[...]
        sin = sin_ref[...]
        for hd in range(bn // HEAD_DIM):
            c = hd * HEAD_DIM
            x1 = acc[:, c:c + HALF]
            x2 = acc[:, c + HALF:c + HEAD_DIM]
            o_ref[:, c:c + HALF] = ((x1 * cos - x2 * sin) * scale).astype(o_ref.dtype)
            o_ref[:, c + HALF:c + HEAD_DIM] = ((x2 * cos + x1 * sin) * scale).astype(o_ref.dtype)

    @pl.when(jnp.logical_and(n >= 2 * nb, n < 3 * nb))
    def _():
        o_ref[...] = project().astype(o_ref.dtype)

    @pl.when(n >= 3 * nb)
    def _():
        o_ref[...] = _silu(project()).astype(o_ref.dtype)


def head_proj(a, w_in, layer, cos, sin):
    s, d = a.shape
    bm, bn = HEAD_ROWS, HEAD_COLS
    nb = RET_WIDTH // bn
    return pl.pallas_call(
        functools.partial(_head_proj_kernel, layer=layer, blocks_per_section=nb),
        grid=(4 * nb, s // bm),
        in_specs=[pl.BlockSpec((bm, d), lambda n, m: (m, 0)),
                  pl.BlockSpec(memory_space=pl.ANY),
                  pl.BlockSpec((bm, HALF), lambda n, m: (m, 0)),
                  pl.BlockSpec((bm, HALF), lambda n, m: (m, 0))],
        out_specs=pl.BlockSpec((bm, bn), lambda n, m: (m, n)),
        out_shape=jax.ShapeDtypeStruct((s, 4 * RET_WIDTH), BF16),
        scratch_shapes=[pltpu.VMEM((d, bn), F32),
                        pltpu.VMEM((d, bn), BF16),
                        pltpu.SemaphoreType.DMA((1,))],
        compiler_params=_params("arbitrary", "arbitrary"),
        name="head_proj",
    )(a, w_in, cos, sin)


def _retention_kernel(lg_ref, q_ref, k_ref, v_ref, sg_ref, gro_ref, o_ref,
                      state_ref, dmask_ref, qd_ref, kd_ref):
    t = RET_BLOCK
    lg = lg_ref[0]

    @pl.when(pl.program_id(1) == 0)
    def _():
        state_ref[...] = jnp.zeros_like(state_ref)
        i = lax.broadcasted_iota(jnp.int32, (t, t), 0)
        j = lax.broadcasted_iota(jnp.int32, (t, t), 1)
        dist = jnp.abs(i - j).astype(F32)
        visible = (j // CHUNK) <= (i // CHUNK)
        dmask_ref[...] = jnp.where(visible, jnp.exp(dist * lg), 0.0)
        fi = i.astype(F32)
        qd_ref[...] = jnp.exp((fi + 1.0) * lg)
        kd_ref[...] = jnp.exp((t - 1.0 - fi) * lg)

    block_decay = jnp.exp(float(t) * lg)
    gro = gro_ref[...]
    for blk in range(RET_ROWS // t):
        rows = pl.ds(blk * t, t)
        q = q_ref[rows, :]
        k = k_ref[rows, :]
        v = v_ref[rows, :]
        scores = lax.dot_general(q, k, (((1,), (1,)), ((), ())), preferred_element_type=F32)
        scores = scores * dmask_ref[...]
        o = jnp.dot(scores.astype(BF16), v, preferred_element_type=F32)
        state = state_ref[...]
        qs = (q.astype(F32) * qd_ref[...]).astype(BF16)
        o = o + jnp.dot(qs, state.astype(BF16), preferred_element_type=F32)
        ks = (k.astype(F32) * kd_ref[...]).astype(BF16)
        kv = lax.dot_general(ks, v, (((0,), (0,)), ((), ())), preferred_element_type=F32)
        state_ref[...] = state * block_decay + kv
        o = o * lax.rsqrt(jnp.mean(o * o, axis=-1, keepdims=True) + EPS)
        o_ref[rows, :] = ((o * gro) * sg_ref[rows, :].astype(F32)).astype(o_ref.dtype)


def retention(qkvg, g_ret_out, log_gamma):
    s = qkvg.shape[0]
    bm = RET_ROWS
    nh = RET_HEADS

    def col(section):
        return pl.BlockSpec((bm, HEAD_DIM), lambda h, i: (i, section * nh + h))

    lg = jnp.broadcast_to(log_gamma[:, None, None], (nh, 1, HEAD_DIM))
    return pl.pallas_call(
        _retention_kernel,
        grid=(nh, s // bm),
        in_specs=[pl.BlockSpec((1, 1, HEAD_DIM), lambda h, i: (h, 0, 0)),
                  col(0), col(1), col(2), col(3),
                  pl.BlockSpec((1, HEAD_DIM), lambda h, i: (0, h))],
        out_specs=pl.BlockSpec((bm, HEAD_DIM), lambda h, i: (i, h)),
        out_shape=jax.ShapeDtypeStruct((s, RET_WIDTH), BF16),
        scratch_shapes=[pltpu.VMEM((HEAD_DIM, HEAD_DIM), F32),
                        pltpu.VMEM((RET_BLOCK, RET_BLOCK), F32),
                        pltpu.VMEM((RET_BLOCK, HEAD_DIM), F32),
                        pltpu.VMEM((RET_BLOCK, HEAD_DIM), F32)],
        compiler_params=_params("arbitrary", "arbitrary"),
        name="retention",
    )(lg, qkvg, qkvg, qkvg, qkvg, g_ret_out.reshape(1, RET_WIDTH))


def _out_proj_kernel(yc_ref, yr_ref, w_ref, gc_ref, h_ref, gf_ref, o_ref, fu_ref, rs_ref,
                     ycn_ref, ss_ref, *, d_model):
    n = pl.program_id(1)
    kc = yc_ref.shape[1]

    @pl.when(n == 0)
    def _():
        yc = yc_ref[...].astype(F32)
        ms = jnp.mean(yc * yc, axis=-1, keepdims=True)
        ycn_ref[...] = (yc * lax.rsqrt(ms + EPS) * gc_ref[...]).astype(BF16)
        ss_ref[...] = jnp.zeros_like(ss_ref)

    acc = jnp.dot(ycn_ref[...], w_ref[0:kc, :], preferred_element_type=F32)
    acc = acc + jnp.dot(yr_ref[...], w_ref[kc:, :], preferred_element_type=F32)
    hn = h_ref[...] + acc
    o_ref[...] = hn
    fu_ref[...] = (hn * gf_ref[...]).astype(fu_ref.dtype)
    ss_ref[...] += jnp.sum(hn * hn, axis=-1, keepdims=True)

    @pl.when(n == pl.num_programs(1) - 1)
    def _():
        rs = lax.rsqrt(ss_ref[...] / d_model + EPS)
        rs_ref[...] = jnp.broadcast_to(rs, rs_ref.shape)


def out_proj(yc, yr, w_out_bf16, g_conv_out, h, g_ffn):
    s, d = h.shape
    bm, bn = OUT_ROWS, OUT_COLS
    kc, kr = yc.shape[1], yr.shape[1]
    return pl.pallas_call(
        functools.partial(_out_proj_kernel, d_model=d),
        grid=(s // bm, d // bn),
        in_specs=[pl.BlockSpec((bm, kc), lambda m, n: (m, 0)),
                  pl.BlockSpec((bm, kr), lambda m, n: (m, 0)),
                  pl.BlockSpec((kc + kr, bn), lambda m, n: (0, n)),
                  pl.BlockSpec((1, kc), lambda m, n: (0, 0)),
                  pl.BlockSpec((bm, bn), lambda m, n: (m, n)),
                  pl.BlockSpec((1, bn), lambda m, n: (0, n))],
        out_specs=[pl.BlockSpec((bm, bn), lambda m, n: (m, n)),
                   pl.BlockSpec((bm, bn), lambda m, n: (m, n)),
                   pl.BlockSpec((bm, LANES), lambda m, n: (m, 0))],
        out_shape=[jax.ShapeDtypeStruct((s, d), F32),
                   jax.ShapeDtypeStruct((s, d), BF16),
                   jax.ShapeDtypeStruct((s, LANES), F32)],
        scratch_shapes=[pltpu.VMEM((bm, kc), BF16), pltpu.VMEM((bm, 1), F32)],
        compiler_params=_params("arbitrary", "arbitrary"),
        name="out_proj",
    )(yc, yr, w_out_bf16, g_conv_out.reshape(1, kc), h, g_ffn.reshape(1, d))


def _swiglu_kernel(a_ref, rs_ref, wg_ref, wu_ref, o_ref, wbf_ref):
    tf = FFN_COLS

    @pl.when(pl.program_id(1) == 0)
    def _():
        wbf_ref[:, 0:tf] = wg_ref[...].astype(BF16)
        wbf_ref[:, tf:2 * tf] = wu_ref[...].astype(BF16)

    acc = jnp.dot(a_ref[...], wbf_ref[...], preferred_element_type=F32) * rs_ref[:, 0:1]
    o_ref[...] = (_silu(acc[:, 0:tf]) * acc[:, tf:2 * tf]).astype(o_ref.dtype)


def swiglu_up(fu, rs, wg, wu, layer):
    s, d = fu.shape
    n = wg.shape[2]
    bm, tf = FFN_ROWS, FFN_COLS
    return pl.pallas_call(
        _swiglu_kernel,
        grid=(n // tf, s // bm),
        in_specs=[pl.BlockSpec((bm, d), lambda j, m: (m, 0)),
                  pl.BlockSpec((bm, LANES), lambda j, m: (m, 0)),
                  pl.BlockSpec((None, d, tf), lambda j, m: (layer, 0, j)),
                  pl.BlockSpec((None, d, tf), lambda j, m: (layer, 0, j))],
        out_specs=pl.BlockSpec((bm, tf), lambda j, m: (m, j)),
        out_shape=jax.ShapeDtypeStruct((s, n), BF16),
        scratch_shapes=[pltpu.VMEM((d, 2 * tf), BF16)],
        compiler_params=_params("arbitrary", "arbitrary"),
        name="swiglu_up",
    )(fu, rs, wg, wu)


def _down_kernel(a_ref, w_ref, h_ref, o_ref):
    o_ref[...] = h_ref[...] + jnp.dot(a_ref[...], w_ref[...], preferred_element_type=F32)


def ffn_down(hid, w, h):
    s, k = hid.shape
    d = w.shape[1]
    bm, bn = DOWN_ROWS, DOWN_COLS
    return pl.pallas_call(
        _down_kernel,
        grid=(s // bm, d // bn),
        in_specs=[pl.BlockSpec((bm, k), lambda i, j: (i, 0)),
                  pl.BlockSpec((k, bn), lambda i, j: (0, j)),
                  pl.BlockSpec((bm, bn), lambda i, j: (i, j))],
        out_specs=pl.BlockSpec((bm, bn), lambda i, j: (i, j)),
        out_shape=jax.ShapeDtypeStruct((s, d), F32),
        compiler_params=_params("arbitrary", "arbitrary"),
        name="ffn_down",
    )(hid, w, h)


def _ple_kernel(h_ref, p_ref, gple_ref, wd_ref, wu_ref, wp_ref, gnext_ref, *out_refs, emit_h):
    h = h_ref[...]
    ms = jnp.mean(h * h, axis=-1, keepdims=True)
    r = (h * lax.rsqrt(ms + EPS) * gple_ref[...]).astype(BF16)
    low = jnp.dot(r, wd_ref[...], preferred_element_type=F32)
    gate = jax.nn.sigmoid(jnp.dot(low.astype(BF16), wu_ref[...], preferred_element_type=F32))
    emb = jnp.dot(p_ref[...].astype(BF16), wp_ref[...], preferred_element_type=F32)
    hn = h + gate * emb
    ms2 = jnp.mean(hn * hn, axis=-1, keepdims=True)
    normed = hn * lax.rsqrt(ms2 + EPS) * gnext_ref[...]
    if emit_h:
        out_refs[0][...] = hn
        out_refs[1][...] = normed.astype(out_refs[1].dtype)
    else:
        out_refs[0][...] = normed.astype(out_refs[0].dtype)


def ple_and_norm(h, p, layer, g_ple, wd, wu, wp, g_next, last):
    s, d = h.shape
    bm = NORM_ROWS
    pdim = p.shape[-1]
    rank = wd.shape[1]
    row = pl.BlockSpec((bm, d), lambda i: (i, 0))
    vec = pl.BlockSpec((1, d), lambda i: (0, 0))
    if last:
        out_specs = row
        out_shape = jax.ShapeDtypeStruct((s, d), F32)
    else:
        out_specs = [row, row]
        out_shape = [jax.ShapeDtypeStruct((s, d), F32), jax.ShapeDtypeStruct((s, d), BF16)]
    return pl.pallas_call(
        functools.partial(_ple_kernel, emit_h=not last),
        grid=(s // bm,),
        in_specs=[row,
                  pl.BlockSpec((None, None, bm, pdim), lambda i: (layer, 0, i, 0)),
                  vec,
                  pl.BlockSpec((d, rank), lambda i: (0, 0)),
                  pl.BlockSpec((rank, d), lambda i: (0, 0)),
                  pl.BlockSpec((pdim, d), lambda i: (0, 0)),
                  vec],
        out_specs=out_specs,
        out_shape=out_shape,
        compiler_params=_params("arbitrary"),
        name="ple",
    )(h, p, g_ple.reshape(1, d), wd, wu, wp, g_next.reshape(1, d))


def kernel(x, p, g_mix, w_in, conv_w, conv_b, g_conv_out, g_ret_out, w_out, g_ffn,
           w_ffn_gate, w_ffn_up, w_ffn_down, g_ple, w_ple_gate_down, w_ple_gate_up,
           w_ple_proj, g_final):
    b, s, d = x.shape
    assert b == 1
    depth = w_in.shape[0]
    h = x.reshape(s, d)

    pos = jnp.arange(s, dtype=F32)
    inv_freq = ROPE_BASE ** (-jnp.arange(HALF, dtype=F32) / HALF)
    ang = pos[:, None] * inv_freq[None, :]
    cos, sin = jnp.cos(ang), jnp.sin(ang)
    log_gamma = jnp.log1p(-jnp.exp2(-5.0 - jnp.arange(RET_HEADS, dtype=F32)))

    a = rmsnorm(h, g_mix[0], BF16)
    for i in range(depth):
        yc = conv_proj(a, w_in, i, conv_w[i], conv_b[i])
        qkvg = head_proj(a, w_in, i, cos, sin)
        yr = retention(qkvg, g_ret_out[i], log_gamma)
        h, fu, rs = out_proj(yc, yr, w_out[i].astype(BF16), g_conv_out[i], h, g_ffn[i])
        hid = swiglu_up(fu, rs, w_ffn_gate, w_ffn_up, i)
        h = ffn_down(hid, w_ffn_down[i].astype(BF16), h)

        last = i == depth - 1
        g_next = g_final if last else g_mix[i + 1]
        res = ple_and_norm(h, p, i, g_ple[i], w_ple_gate_down[i].astype(BF16),
                           w_ple_gate_up[i].astype(BF16), w_ple_proj[i].astype(BF16),
                           g_next, last)
        if last:
            return res.reshape(b, s, d)
        h, a = res
```

```python
import functools

import jax
import jax.numpy as jnp
from jax import lax
from jax.experimental import pallas as pl
from jax.experimental.pallas import tpu as pltpu

EPS = 1e-6
CHUNK = 64
CONV_WIDTH = 2048
RET_WIDTH = 2048
RET_HEADS = 8
HEAD_DIM = 256
HALF = HEAD_DIM // 2
ROPE_BASE = 10000.0

F32 = jnp.float32
BF16 = jnp.bfloat16

SUBLANES = 8
LANES = 128
V7X_VMEM_LIMIT = 62 * 1024 * 1024

CONV_ROWS = 1024
CONV_COLS = 256
HEAD_ROWS = 1024
HEAD_COLS = 1024
OUT_ROWS = 1024
OUT_COLS = 512
CAST_ROWS = 512
FFN_ROWS = 2048
FFN_COLS = 256
DOWN_ROWS = 512
DOWN_COLS = 512
RET_ROWS = 2048
RET_BLOCK = 256
NORM_ROWS = 256


def _params(*sem):
    return pltpu.CompilerParams(dimension_semantics=sem, vmem_limit_bytes=V7X_VMEM_LIMIT)


def _silu(x):
    return x * jax.nn.sigmoid(x)


def _rmsnorm_kernel(x_ref, g_ref, o_ref):
    x = x_ref[...]
    ms = jnp.mean(x * x, axis=-1, keepdims=True)
    o_ref[...] = (x * lax.rsqrt(ms + EPS) * g_ref[...]).astype(o_ref.dtype)


def rmsnorm(x, g, out_dtype):
    s, d = x.shape
    bm = NORM_ROWS
    return pl.pallas_call(
        _rmsnorm_kernel,
        grid=(s // bm,),
        in_specs=[pl.BlockSpec((bm, d), lambda i: (i, 0)),
                  pl.BlockSpec((1, d), lambda i: (0, 0))],
        out_specs=pl.BlockSpec((bm, d), lambda i: (i, 0)),
        out_shape=jax.ShapeDtypeStruct((s, d), out_dtype),
        compiler_params=_params("arbitrary"),
        name="rmsnorm",
    )(x, g.reshape(1, d))


def _round_kernel(x_ref, o_ref):
    o_ref[...] = x_ref[...].astype(o_ref.dtype)


def round_to_bf16(w):
    depth, k, n = w.shape
    rows = depth * k
    bm = CAST_ROWS
    assert rows % bm == 0
    out = pl.pallas_call(
        _round_kernel,
        grid=(rows // bm,),
        in_specs=[pl.BlockSpec((bm, n), lambda i: (i, 0))],
        out_specs=pl.BlockSpec((bm, n), lambda i: (i, 0)),
        out_shape=jax.ShapeDtypeStruct((rows, n), BF16),
        compiler_params=_params("arbitrary"),
        name="round_to_bf16",
    )(w.reshape(rows, n))
    return out.reshape(depth, k, n)


def _prefetched_weight_tile(w_hbm, windows, stage_ref, wbf_ref, sems):
    n = pl.program_id(0)

    def copies(j):
        return [pltpu.make_async_copy(src, dst, sems.at[i])
                for i, (src, dst) in enumerate(windows(j))]

    @pl.when(n == 0)
    def _():
        for c in copies(0):
            c.start()

    for c in copies(n):
        c.wait()
    wbf_ref[...] = stage_ref[...].astype(BF16)

    @pl.when(n + 1 < pl.num_programs(0))
    def _():
        for c in copies(n + 1):
            c.start()


def _conv_proj_kernel(a_ref, w_hbm, cw_ref, cbias_ref, o_ref, stage_ref, wbf_ref, halo_ref, sems,
                      *, layer):
    tc = CONV_COLS

    def windows(j):
        return [(w_hbm.at[layer, :, pl.ds(sec * CONV_WIDTH + j * tc, tc)],
                 stage_ref.at[:, pl.ds(sec * tc, tc)]) for sec in range(3)]

    @pl.when(pl.program_id(1) == 0)
    def _():
        _prefetched_weight_tile(w_hbm, windows, stage_ref, wbf_ref, sems)
        halo_ref[...] = jnp.zeros_like(halo_ref)

    acc = jnp.dot(a_ref[...], wbf_ref[...], preferred_element_type=F32)
    cb = acc[:, 0:tc]
    u = acc[:, tc:2 * tc] * acc[:, 2 * tc:3 * tc]
    bm = u.shape[0]
    row = lax.broadcasted_iota(jnp.int32, (bm, tc), 0)
    halo = halo_ref[...]
    prev1 = halo[SUBLANES - 1:SUBLANES, :]
    prev2 = halo[SUBLANES - 2:SUBLANES - 1, :]
    u1 = jnp.where(row == 0, prev1, pltpu.roll(u, 1, axis=0))
    u2 = jnp.where(row == 0, prev2, jnp.where(row == 1, prev1, pltpu.roll(u, 2, axis=0)))
    halo_ref[...] = u[bm - SUBLANES:, :]
    cw = cw_ref[...]
    conv = u2 * cw[0:1, :] + u1 * cw[1:2, :] + u * cw[2:3, :] + cbias_ref[...]
    o_ref[...] = (cb * conv).astype(o_ref.dtype)


def conv_proj(a, w_in, layer, conv_w, conv_b):
    s, d = a.shape
    bm, tc = CONV_ROWS, CONV_COLS
    return pl.pallas_call(
        functools.partial(_conv_proj_kernel, layer=layer),
        grid=(CONV_WIDTH // tc, s // bm),
        in_specs=[pl.BlockSpec((bm, d), lambda n, m: (m, 0)),
                  pl.BlockSpec(memory_space=pl.ANY),
                  pl.BlockSpec((3, tc), lambda n, m: (0, n)),
                  pl.BlockSpec((1, tc), lambda n, m: (0, n))],
        out_specs=pl.BlockSpec((bm, tc), lambda n, m: (m, n)),
        out_shape=jax.ShapeDtypeStruct((s, CONV_WIDTH), BF16),
        scratch_shapes=[pltpu.VMEM((d, 3 * tc), F32),
                        pltpu.VMEM((d, 3 * tc), BF16),
                        pltpu.VMEM((SUBLANES, tc), F32),
                        pltpu.SemaphoreType.DMA((3,))],
        compiler_params=_params("arbitrary", "arbitrary"),
        name="conv_proj",
    )(a, w_in, conv_w, conv_b.reshape(1, CONV_WIDTH))


def _head_proj_kernel(a_ref, w_hbm, cos_ref, sin_ref, o_ref, stage_ref, wbf_ref, sems,
                      *, layer, blocks_per_section):
    n = pl.program_id(0)
    nb = blocks_per_section
    bn = HEAD_COLS

    def windows(j):
        return [(w_hbm.at[layer, :, pl.ds(3 * CONV_WIDTH + j * bn, bn)], stage_ref)]

    @pl.when(pl.program_id(1) == 0)
    def _():
        _prefetched_weight_tile(w_hbm, windows, stage_ref, wbf_ref, sems)

    def project():
        return jnp.dot(a_ref[...], wbf_ref[...], preferred_element_type=F32)

    @pl.when(n < 2 * nb)
    def _():
        acc = project()
        scale = jnp.where(n < nb, HEAD_DIM ** -0.5, 1.0).astype(F32)
        cos = cos_ref[...]
        sin = sin_ref[...]
        for hd in range(bn // HEAD_DIM):
            c = hd * HEAD_DIM
            x1 = acc[:, c:c + HALF]
            x2 = acc[:, c + HALF:c + HEAD_DIM]
            o_ref[:, c:c + HALF] = ((x1 * cos - x2 * sin) * scale).astype(o_ref.dtype)
            o_ref[:, c + HALF:c + HEAD_DIM] = ((x2 * cos + x1 * sin) * scale).astype(o_ref.dtype)

    @pl.when(jnp.logical_and(n >= 2 * nb, n < 3 * nb))
    def _():
        o_ref[...] = project().astype(o_ref.dtype)

    @pl.when(n >= 3 * nb)
    def _():
        o_ref[...] = _silu(project()).astype(o_ref.dtype)


def head_proj(a, w_in, layer, cos, sin):
    s, d = a.shape
    bm, bn = HEAD_ROWS, HEAD_COLS
    nb = RET_WIDTH // bn
    return pl.pallas_call(
        functools.partial(_head_proj_kernel, layer=layer, blocks_per_section=nb),
        grid=(4 * nb, s // bm),
        in_specs=[pl.BlockSpec((bm, d), lambda n, m: (m, 0)),
                  pl.BlockSpec(memory_space=pl.ANY),
                  pl.BlockSpec((bm, HALF), lambda n, m: (m, 0)),
                  pl.BlockSpec((bm, HALF), lambda n, m: (m, 0))],
        out_specs=pl.BlockSpec((bm, bn), lambda n, m: (m, n)),
        out_shape=jax.ShapeDtypeStruct((s, 4 * RET_WIDTH), BF16),
        scratch_shapes=[pltpu.VMEM((d, bn), F32),
                        pltpu.VMEM((d, bn), BF16),
                        pltpu.SemaphoreType.DMA((1,))],
        compiler_params=_params("arbitrary", "arbitrary"),
        name="head_proj",
    )(a, w_in, cos, sin)


def _retention_kernel(lg_ref, q_ref, k_ref, v_ref, sg_ref, gro_ref, o_ref,
                      state_ref, dmask_ref, qd_ref, kd_ref):
    t = RET_BLOCK
    lg = lg_ref[0]

    @pl.when(pl.program_id(1) == 0)
    def _():
        state_ref[...] = jnp.zeros_like(state_ref)
        i = lax.broadcasted_iota(jnp.int32, (t, t), 0)
        j = lax.broadcasted_iota(jnp.int32, (t, t), 1)
        dist = jnp.abs(i - j).astype(F32)
        visible = (j // CHUNK) <= (i // CHUNK)
        dmask_ref[...] = jnp.where(visible, jnp.exp(dist * lg), 0.0)
        fi = i.astype(F32)
        qd_ref[...] = jnp.exp((fi + 1.0) * lg)
        kd_ref[...] = jnp.exp((t - 1.0 - fi) * lg)

    block_decay = jnp.exp(float(t) * lg)
    gro = gro_ref[...]
    for blk in range(RET_ROWS // t):
        rows = pl.ds(blk * t, t)
        q = q_ref[rows, :]
        k = k_ref[rows, :]
        v = v_ref[rows, :]
        scores = lax.dot_general(q, k, (((1,), (1,)), ((), ())), preferred_element_type=F32)
        scores = scores * dmask_ref[...]
        o = jnp.dot(scores.astype(BF16), v, preferred_element_type=F32)
        state = state_ref[...]
        qs = (q.astype(F32) * qd_ref[...]).astype(BF16)
        o = o + jnp.dot(qs, state.astype(BF16), preferred_element_type=F32)
        ks = (k.astype(F32) * kd_ref[...]).astype(BF16)
        kv = lax.dot_general(ks, v, (((0,), (0,)), ((), ())), preferred_element_type=F32)
        state_ref[...] = state * block_decay + kv
        o = o * lax.rsqrt(jnp.mean(o * o, axis=-1, keepdims=True) + EPS)
        o_ref[rows, :] = ((o * gro) * sg_ref[rows, :].astype(F32)).astype(o_ref.dtype)


def retention(qkvg, g_ret_out, log_gamma):
    s = qkvg.shape[0]
    bm = RET_ROWS
    nh = RET_HEADS

    def col(section):
        return pl.BlockSpec((bm, HEAD_DIM), lambda h, i: (i, section * nh + h))

    lg = jnp.broadcast_to(log_gamma[:, None, None], (nh, 1, HEAD_DIM))
    return pl.pallas_call(
        _retention_kernel,
        grid=(nh, s // bm),
        in_specs=[pl.BlockSpec((1, 1, HEAD_DIM), lambda h, i: (h, 0, 0)),
                  col(0), col(1), col(2), col(3),
                  pl.BlockSpec((1, HEAD_DIM), lambda h, i: (0, h))],
        out_specs=pl.BlockSpec((bm, HEAD_DIM), lambda h, i: (i, h)),
        out_shape=jax.ShapeDtypeStruct((s, RET_WIDTH), BF16),
        scratch_shapes=[pltpu.VMEM((HEAD_DIM, HEAD_DIM), F32),
                        pltpu.VMEM((RET_BLOCK, RET_BLOCK), F32),
                        pltpu.VMEM((RET_BLOCK, HEAD_DIM), F32),
                        pltpu.VMEM((RET_BLOCK, HEAD_DIM), F32)],
        compiler_params=_params("arbitrary", "arbitrary"),
        name="retention",
    )(lg, qkvg, qkvg, qkvg, qkvg, g_ret_out.reshape(1, RET_WIDTH))


def _out_proj_kernel(yc_ref, yr_ref, w_ref, gc_ref, h_ref, gf_ref, o_ref, fu_ref, rs_ref,
                     ycn_ref, ss_ref, *, d_model):
    n = pl.program_id(1)
    kc = yc_ref.shape[1]

    @pl.when(n == 0)
    def _():
        yc = yc_ref[...].astype(F32)
        ms = jnp.mean(yc * yc, axis=-1, keepdims=True)
        ycn_ref[...] = (yc * lax.rsqrt(ms + EPS) * gc_ref[...]).astype(BF16)
        ss_ref[...] = jnp.zeros_like(ss_ref)

    acc = jnp.dot(ycn_ref[...], w_ref[0:kc, :], preferred_element_type=F32)
    acc = acc + jnp.dot(yr_ref[...], w_ref[kc:, :], preferred_element_type=F32)
    hn = h_ref[...] + acc
    o_ref[...] = hn
    fu_ref[...] = (hn * gf_ref[...]).astype(fu_ref.dtype)
    ss_ref[...] += jnp.sum(hn * hn, axis=-1, keepdims=True)

    @pl.when(n == pl.num_programs(1) - 1)
    def _():
        rs = lax.rsqrt(ss_ref[...] / d_model + EPS)
        rs_ref[...] = jnp.broadcast_to(rs, rs_ref.shape)


def out_proj(yc, yr, w_out_bf16, layer, g_conv_out, h, g_ffn):
    s, d = h.shape
    bm, bn = OUT_ROWS, OUT_COLS
    kc, kr = yc.shape[1], yr.shape[1]
    return pl.pallas_call(
        functools.partial(_out_proj_kernel, d_model=d),
        grid=(s // bm, d // bn),
        in_specs=[pl.BlockSpec((bm, kc), lambda m, n: (m, 0)),
                  pl.BlockSpec((bm, kr), lambda m, n: (m, 0)),
                  pl.BlockSpec((None, kc + kr, bn), lambda m, n: (layer, 0, n)),
                  pl.BlockSpec((1, kc), lambda m, n: (0, 0)),
                  pl.BlockSpec((bm, bn), lambda m, n: (m, n)),
                  pl.BlockSpec((1, bn), lambda m, n: (0, n))],
        out_specs=[pl.BlockSpec((bm, bn), lambda m, n: (m, n)),
                   pl.BlockSpec((bm, bn), lambda m, n: (m, n)),
                   pl.BlockSpec((bm, LANES), lambda m, n: (m, 0))],
        out_shape=[jax.ShapeDtypeStruct((s, d), F32),
                   jax.ShapeDtypeStruct((s, d), BF16),
                   jax.ShapeDtypeStruct((s, LANES), F32)],
        scratch_shapes=[pltpu.VMEM((bm, kc), BF16), pltpu.VMEM((bm, 1), F32)],
        compiler_params=_params("arbitrary", "arbitrary"),
        name="out_proj",
    )(yc, yr, w_out_bf16, g_conv_out.reshape(1, kc), h, g_ffn.reshape(1, d))


def _swiglu_kernel(fu_hbm, rs_hbm, wg_ref, wu_ref, o_hbm, a_buf, rs_buf, o_buf, wbf_ref,
                   in_sems, out_sems, *, row_tiles):
    tf, bm = FFN_COLS, FFN_ROWS
    j = pl.program_id(0)
    last_step = j == pl.num_programs(0) - 1

    def in_copies(i, slot):
        rows = pl.ds(i * bm, bm)
        return [pltpu.make_async_copy(fu_hbm.at[rows, :], a_buf.at[slot], in_sems.at[0, slot]),
                pltpu.make_async_copy(rs_hbm.at[rows, :], rs_buf.at[slot], in_sems.at[1, slot])]

    def out_copy(i, slot):
        return pltpu.make_async_copy(o_buf.at[slot],
                                     o_hbm.at[pl.ds(i * bm, bm), pl.ds(j * tf, tf)],
                                     out_sems.at[slot])

    @pl.when(j == 0)
    def _():
        for c in in_copies(0, 0):
            c.start()

    wbf_ref[:, 0:tf] = wg_ref[...].astype(BF16)
    wbf_ref[:, tf:2 * tf] = wu_ref[...].astype(BF16)

    def tile(i, slot):
        for c in in_copies(i, slot):
            c.wait()
        more_rows = i + 1 < row_tiles
        next_tile = jnp.where(more_rows, i + 1, 0)

        @pl.when(jnp.logical_or(more_rows, jnp.logical_not(last_step)))
        def _():
            for c in in_copies(next_tile, 1 - slot):
                c.start()

        @pl.when(j * row_tiles + i >= 2)
        def _():
            out_copy(i, slot).wait()

        acc = jnp.dot(a_buf[slot], wbf_ref[...], preferred_element_type=F32) * rs_buf[slot, :, 0:1]
        o_buf[slot] = (_silu(acc[:, 0:tf]) * acc[:, tf:2 * tf]).astype(o_buf.dtype)
        out_copy(i, slot).start()

    def pair(k, carry):
        tile(2 * k, 0)
        tile(2 * k + 1, 1)
        return carry

    lax.fori_loop(0, row_tiles // 2, pair, 0)

    @pl.when(last_step)
    def _():
        out_copy(row_tiles - 2, 0).wait()
        out_copy(row_tiles - 1, 1).wait()


def swiglu_up(fu, rs, wg, wu, layer):
    s, d = fu.shape
    n = wg.shape[2]
    bm, tf = FFN_ROWS, FFN_COLS
    row_tiles = s // bm
    assert row_tiles % 2 == 0 and row_tiles * bm == s
    return pl.pallas_call(
        functools.partial(_swiglu_kernel, row_tiles=row_tiles),
        grid=(n // tf,),
        in_specs=[pl.BlockSpec(memory_space=pl.ANY),
                  pl.BlockSpec(memory_space=pl.ANY),
                  pl.BlockSpec((None, d, tf), lambda j: (layer, 0, j)),
                  pl.BlockSpec((None, d, tf), lambda j: (layer, 0, j))],
        out_specs=pl.BlockSpec(memory_space=pl.ANY),
        out_shape=jax.ShapeDtypeStruct((s, n), BF16),
        scratch_shapes=[pltpu.VMEM((2, bm, d), BF16),
                        pltpu.VMEM((2, bm, LANES), F32),
                        pltpu.VMEM((2, bm, tf), BF16),
                        pltpu.VMEM((d, 2 * tf), BF16),
                        pltpu.SemaphoreType.DMA((2, 2)),
                        pltpu.SemaphoreType.DMA((2,))],
        compiler_params=_params("arbitrary"),
        name="swiglu_up",
    )(fu, rs, wg, wu)


def _down_kernel(a_ref, w_ref, h_ref, o_ref):
    o_ref[...] = h_ref[...] + jnp.dot(a_ref[...], w_ref[...], preferred_element_type=F32)


def ffn_down(hid, w_bf16, layer, h):
    s, k = hid.shape
    d = w_bf16.shape[2]
    bm, bn = DOWN_ROWS, DOWN_COLS
    return pl.pallas_call(
        _down_kernel,
        grid=(s // bm, d // bn),
        in_specs=[pl.BlockSpec((bm, k), lambda i, j: (i, 0)),
                  pl.BlockSpec((None, k, bn), lambda i, j: (layer, 0, j)),
                  pl.BlockSpec((bm, bn), lambda i, j: (i, j))],
        out_specs=pl.BlockSpec((bm, bn), lambda i, j: (i, j)),
        out_shape=jax.ShapeDtypeStruct((s, d), F32),
        compiler_params=_params("arbitrary", "arbitrary"),
        name="ffn_down",
    )(hid, w_bf16, h)


def _ple_kernel(h_ref, p_ref, gple_ref, wd_ref, wu_ref, wp_ref, gnext_ref, *out_refs, emit_h):
    h = h_ref[...]
    ms = jnp.mean(h * h, axis=-1, keepdims=True)
    r = (h * lax.rsqrt(ms + EPS) * gple_ref[...]).astype(BF16)
    low = jnp.dot(r, wd_ref[...], preferred_element_type=F32)
    gate = jax.nn.sigmoid(jnp.dot(low.astype(BF16), wu_ref[...], preferred_element_type=F32))
    emb = jnp.dot(p_ref[...].astype(BF16), wp_ref[...], preferred_element_type=F32)
    hn = h + gate * emb
    ms2 = jnp.mean(hn * hn, axis=-1, keepdims=True)
    normed = hn * lax.rsqrt(ms2 + EPS) * gnext_ref[...]
    if emit_h:
        out_refs[0][...] = hn
        out_refs[1][...] = normed.astype(out_refs[1].dtype)
    else:
        out_refs[0][...] = normed.astype(out_refs[0].dtype)


def ple_and_norm(h, p, layer, g_ple, wd, wu, wp, g_next, last):
    s, d = h.shape
    bm = NORM_ROWS
    pdim = p.shape[-1]
    rank = wd.shape[1]
    row = pl.BlockSpec((bm, d), lambda i: (i, 0))
    vec = pl.BlockSpec((1, d), lambda i: (0, 0))
    if last:
        out_specs = row
        out_shape = jax.ShapeDtypeStruct((s, d), F32)
    else:
        out_specs = [row, row]
        out_shape = [jax.ShapeDtypeStruct((s, d), F32), jax.ShapeDtypeStruct((s, d), BF16)]
    return pl.pallas_call(
        functools.partial(_ple_kernel, emit_h=not last),
        grid=(s // bm,),
        in_specs=[row,
                  pl.BlockSpec((None, None, bm, pdim), lambda i: (layer, 0, i, 0)),
                  vec,
                  pl.BlockSpec((d, rank), lambda i: (0, 0)),
                  pl.BlockSpec((rank, d), lambda i: (0, 0)),
                  pl.BlockSpec((pdim, d), lambda i: (0, 0)),
                  vec],
        out_specs=out_specs,
        out_shape=out_shape,
        compiler_params=_params("arbitrary"),
        name="ple",
    )(h, p, g_ple.reshape(1, d), wd, wu, wp, g_next.reshape(1, d))


def kernel(x, p, g_mix, w_in, conv_w, conv_b, g_conv_out, g_ret_out, w_out, g_ffn,
           w_ffn_gate, w_ffn_up, w_ffn_down, g_ple, w_ple_gate_down, w_ple_gate_up,
           w_ple_proj, g_final):
    b, s, d = x.shape
    assert b == 1
    depth = w_in.shape[0]
    h = x.reshape(s, d)

    pos = jnp.arange(s, dtype=F32)
    inv_freq = ROPE_BASE ** (-jnp.arange(HALF, dtype=F32) / HALF)
    ang = pos[:, None] * inv_freq[None, :]
    cos, sin = jnp.cos(ang), jnp.sin(ang)
    log_gamma = jnp.log1p(-jnp.exp2(-5.0 - jnp.arange(RET_HEADS, dtype=F32)))

    w_out_bf16 = round_to_bf16(w_out)
    w_down_bf16 = round_to_bf16(w_ffn_down)

    a = rmsnorm(h, g_mix[0], BF16)
    for i in range(depth):
        yc = conv_proj(a, w_in, i, conv_w[i], conv_b[i])
        qkvg = head_proj(a, w_in, i, cos, sin)
        yr = retention(qkvg, g_ret_out[i], log_gamma)
        h, fu, rs = out_proj(yc, yr, w_out_bf16, i, g_conv_out[i], h, g_ffn[i])
        hid = swiglu_up(fu, rs, w_ffn_gate, w_ffn_up, i)
        h = ffn_down(hid, w_down_bf16, i, h)

        last = i == depth - 1
        g_next = g_final if last else g_mix[i + 1]
        res = ple_and_norm(h, p, i, g_ple[i], w_ple_gate_down[i].astype(BF16),
                           w_ple_gate_up[i].astype(BF16), w_ple_proj[i].astype(BF16),
                           g_next, last)
        if last:
            return res.reshape(b, s, d)
        h, a = res
```

```python
import functools

import jax
import jax.numpy as jnp
from jax import lax
from jax.experimental import pallas as pl
from jax.experimental.pallas import tpu as pltpu

EPS = 1e-6
CHUNK = 64
CONV_WIDTH = 2048
RET_WIDTH = 2048
RET_HEADS = 8
HEAD_DIM = 256
HALF = HEAD_DIM // 2
ROPE_BASE = 10000.0

F32 = jnp.float32
BF16 = jnp.bfloat16

SUBLANES = 8
LANES = 128
V7X_VMEM_LIMIT = 62 * 1024 * 1024

CONV_ROWS = 1024
CONV_COLS = 256
HEAD_ROWS = 1024
HEAD_COLS = 1024
OUT_ROWS = 1024
OUT_COLS = 512
CAST_ROWS = 512
FFN_ROWS = 2048
FFN_COLS = 256
DOWN_ROWS = 512
DOWN_COLS = 512
RET_ROWS = 2048
RET_BLOCK = 256
NORM_ROWS = 256
PLE_ROW_GROUPS = 2


def _params(*sem):
    return pltpu.CompilerParams(dimension_semantics=sem, vmem_limit_bytes=V7X_VMEM_LIMIT)


def _silu(x):
    return x * jax.nn.sigmoid(x)


def _rmsnorm_kernel(x_ref, g_ref, o_ref):
    x = x_ref[...]
    ms = jnp.mean(x * x, axis=-1, keepdims=True)
    o_ref[...] = (x * lax.rsqrt(ms + EPS) * g_ref[...]).astype(o_ref.dtype)


def rmsnorm(x, g, out_dtype):
    s, d = x.shape
    bm = NORM_ROWS
    return pl.pallas_call(
        _rmsnorm_kernel,
        grid=(s // bm,),
        in_specs=[pl.BlockSpec((bm, d), lambda i: (i, 0)),
                  pl.BlockSpec((1, d), lambda i: (0, 0))],
        out_specs=pl.BlockSpec((bm, d), lambda i: (i, 0)),
        out_shape=jax.ShapeDtypeStruct((s, d), out_dtype),
        compiler_params=_params("arbitrary"),
        name="rmsnorm",
    )(x, g.reshape(1, d))


def _round_kernel(x_ref, o_ref):
    o_ref[...] = x_ref[...].astype(o_ref.dtype)


def round_to_bf16(w):
    depth, k, n = w.shape
    rows = depth * k
    bm = CAST_ROWS
    assert rows % bm == 0
    out = pl.pallas_call(
        _round_kernel,
        grid=(rows // bm,),
        in_specs=[pl.BlockSpec((bm, n), lambda i: (i, 0))],
        out_specs=pl.BlockSpec((bm, n), lambda i: (i, 0)),
        out_shape=jax.ShapeDtypeStruct((rows, n), BF16),
        compiler_params=_params("arbitrary"),
        name="round_to_bf16",
    )(w.reshape(rows, n))
    return out.reshape(depth, k, n)


def _prefetched_weight_tile(w_hbm, windows, stage_ref, wbf_ref, sems):
    n = pl.program_id(0)

    def copies(j):
        return [pltpu.make_async_copy(src, dst, sems.at[i])
                for i, (src, dst) in enumerate(windows(j))]

    @pl.when(n == 0)
    def _():
        for c in copies(0):
            c.start()

    for c in copies(n):
        c.wait()
    wbf_ref[...] = stage_ref[...].astype(BF16)

    @pl.when(n + 1 < pl.num_programs(0))
    def _():
        for c in copies(n + 1):
            c.start()


def _conv_proj_kernel(a_ref, w_hbm, cw_ref, cbias_ref, o_ref, stage_ref, wbf_ref, halo_ref, sems,
                      *, layer):
    tc = CONV_COLS

    def windows(j):
        return [(w_hbm.at[layer, :, pl.ds(sec * CONV_WIDTH + j * tc, tc)],
                 stage_ref.at[:, pl.ds(sec * tc, tc)]) for sec in range(3)]

    @pl.when(pl.program_id(1) == 0)
    def _():
        _prefetched_weight_tile(w_hbm, windows, stage_ref, wbf_ref, sems)
        halo_ref[...] = jnp.zeros_like(halo_ref)

    acc = jnp.dot(a_ref[...], wbf_ref[...], preferred_element_type=F32)
    cb = acc[:, 0:tc]
    u = acc[:, tc:2 * tc] * acc[:, 2 * tc:3 * tc]
    bm = u.shape[0]
    row = lax.broadcasted_iota(jnp.int32, (bm, tc), 0)
    halo = halo_ref[...]
    prev1 = halo[SUBLANES - 1:SUBLANES, :]
    prev2 = halo[SUBLANES - 2:SUBLANES - 1, :]
    u1 = jnp.where(row == 0, prev1, pltpu.roll(u, 1, axis=0))
    u2 = jnp.where(row == 0, prev2, jnp.where(row == 1, prev1, pltpu.roll(u, 2, axis=0)))
    halo_ref[...] = u[bm - SUBLANES:, :]
    cw = cw_ref[...]
    conv = u2 * cw[0:1, :] + u1 * cw[1:2, :] + u * cw[2:3, :] + cbias_ref[...]
    o_ref[...] = (cb * conv).astype(o_ref.dtype)


def conv_proj(a, w_in, layer, conv_w, conv_b):
    s, d = a.shape
    bm, tc = CONV_ROWS, CONV_COLS
    return pl.pallas_call(
        functools.partial(_conv_proj_kernel, layer=layer),
        grid=(CONV_WIDTH // tc, s // bm),
        in_specs=[pl.BlockSpec((bm, d), lambda n, m: (m, 0)),
                  pl.BlockSpec(memory_space=pl.ANY),
                  pl.BlockSpec((3, tc), lambda n, m: (0, n)),
                  pl.BlockSpec((1, tc), lambda n, m: (0, n))],
        out_specs=pl.BlockSpec((bm, tc), lambda n, m: (m, n)),
        out_shape=jax.ShapeDtypeStruct((s, CONV_WIDTH), BF16),
        scratch_shapes=[pltpu.VMEM((d, 3 * tc), F32),
                        pltpu.VMEM((d, 3 * tc), BF16),
                        pltpu.VMEM((SUBLANES, tc), F32),
                        pltpu.SemaphoreType.DMA((3,))],
        compiler_params=_params("arbitrary", "arbitrary"),
        name="conv_proj",
    )(a, w_in, conv_w, conv_b.reshape(1, CONV_WIDTH))


def _head_proj_kernel(a_ref, w_hbm, cos_ref, sin_ref, o_ref, stage_ref, wbf_ref, sems,
                      *, layer, blocks_per_section):
    n = pl.program_id(0)
    nb = blocks_per_section
    bn = HEAD_COLS

    def windows(j):
        return [(w_hbm.at[layer, :, pl.ds(3 * CONV_WIDTH + j * bn, bn)], stage_ref)]

    @pl.when(pl.program_id(1) == 0)
    def _():
        _prefetched_weight_tile(w_hbm, windows, stage_ref, wbf_ref, sems)

    def project():
        return jnp.dot(a_ref[...], wbf_ref[...], preferred_element_type=F32)

    @pl.when(n < 2 * nb)
    def _():
        acc = project()
        scale = jnp.where(n < nb, HEAD_DIM ** -0.5, 1.0).astype(F32)
        cos = cos_ref[...]
        sin = sin_ref[...]
        for hd in range(bn // HEAD_DIM):
            c = hd * HEAD_DIM
            x1 = acc[:, c:c + HALF]
            x2 = acc[:, c + HALF:c + HEAD_DIM]
            o_ref[:, c:c + HALF] = ((x1 * cos - x2 * sin) * scale).astype(o_ref.dtype)
            o_ref[:, c + HALF:c + HEAD_DIM] = ((x2 * cos + x1 * sin) * scale).astype(o_ref.dtype)

    @pl.when(jnp.logical_and(n >= 2 * nb, n < 3 * nb))
    def _():
        o_ref[...] = project().astype(o_ref.dtype)

    @pl.when(n >= 3 * nb)
    def _():
        o_ref[...] = _silu(project()).astype(o_ref.dtype)


def head_proj(a, w_in, layer, cos, sin):
    s, d = a.shape
    bm, bn = HEAD_ROWS, HEAD_COLS
    nb = RET_WIDTH // bn
    return pl.pallas_call(
        functools.partial(_head_proj_kernel, layer=layer, blocks_per_section=nb),
        grid=(4 * nb, s // bm),
        in_specs=[pl.BlockSpec((bm, d), lambda n, m: (m, 0)),
                  pl.BlockSpec(memory_space=pl.ANY),
                  pl.BlockSpec((bm, HALF), lambda n, m: (m, 0)),
                  pl.BlockSpec((bm, HALF), lambda n, m: (m, 0))],
        out_specs=pl.BlockSpec((bm, bn), lambda n, m: (m, n)),
        out_shape=jax.ShapeDtypeStruct((s, 4 * RET_WIDTH), BF16),
        scratch_shapes=[pltpu.VMEM((d, bn), F32),
                        pltpu.VMEM((d, bn), BF16),
                        pltpu.SemaphoreType.DMA((1,))],
        compiler_params=_params("arbitrary", "arbitrary"),
        name="head_proj",
    )(a, w_in, cos, sin)


def _retention_kernel(lg_ref, q_ref, k_ref, v_ref, sg_ref, gro_ref, o_ref,
                      state_ref, dmask_ref, qd_ref, kd_ref):
    t = RET_BLOCK
    lg = lg_ref[0]

    @pl.when(pl.program_id(1) == 0)
    def _():
        state_ref[...] = jnp.zeros_like(state_ref)
        i = lax.broadcasted_iota(jnp.int32, (t, t), 0)
        j = lax.broadcasted_iota(jnp.int32, (t, t), 1)
        dist = jnp.abs(i - j).astype(F32)
        visible = (j // CHUNK) <= (i // CHUNK)
        dmask_ref[...] = jnp.where(visible, jnp.exp(dist * lg), 0.0)
        fi = i.astype(F32)
        qd_ref[...] = jnp.exp((fi + 1.0) * lg)
        kd_ref[...] = jnp.exp((t - 1.0 - fi) * lg)

    block_decay = jnp.exp(float(t) * lg)
    gro = gro_ref[...]
    for blk in range(RET_ROWS // t):
        rows = pl.ds(blk * t, t)
        q = q_ref[rows, :]
        k = k_ref[rows, :]
        v = v_ref[rows, :]
        scores = lax.dot_general(q, k, (((1,), (1,)), ((), ())), preferred_element_type=F32)
        scores = scores * dmask_ref[...]
        o = jnp.dot(scores.astype(BF16), v, preferred_element_type=F32)
        state = state_ref[...]
        qs = (q.astype(F32) * qd_ref[...]).astype(BF16)
        o = o + jnp.dot(qs, state.astype(BF16), preferred_element_type=F32)
        ks = (k.astype(F32) * kd_ref[...]).astype(BF16)
        kv = lax.dot_general(ks, v, (((0,), (0,)), ((), ())), preferred_element_type=F32)
        state_ref[...] = state * block_decay + kv
        o = o * lax.rsqrt(jnp.mean(o * o, axis=-1, keepdims=True) + EPS)
        o_ref[rows, :] = ((o * gro) * sg_ref[rows, :].astype(F32)).astype(o_ref.dtype)


def retention(qkvg, g_ret_out, log_gamma):
    s = qkvg.shape[0]
    bm = RET_ROWS
    nh = RET_HEADS

    def col(section):
        return pl.BlockSpec((bm, HEAD_DIM), lambda h, i: (i, section * nh + h))

    lg = jnp.broadcast_to(log_gamma[:, None, None], (nh, 1, HEAD_DIM))
    return pl.pallas_call(
        _retention_kernel,
        grid=(nh, s // bm),
        in_specs=[pl.BlockSpec((1, 1, HEAD_DIM), lambda h, i: (h, 0, 0)),
                  col(0), col(1), col(2), col(3),
                  pl.BlockSpec((1, HEAD_DIM), lambda h, i: (0, h))],
        out_specs=pl.BlockSpec((bm, HEAD_DIM), lambda h, i: (i, h)),
        out_shape=jax.ShapeDtypeStruct((s, RET_WIDTH), BF16),
        scratch_shapes=[pltpu.VMEM((HEAD_DIM, HEAD_DIM), F32),
                        pltpu.VMEM((RET_BLOCK, RET_BLOCK), F32),
                        pltpu.VMEM((RET_BLOCK, HEAD_DIM), F32),
                        pltpu.VMEM((RET_BLOCK, HEAD_DIM), F32)],
        compiler_params=_params("arbitrary", "arbitrary"),
        name="retention",
    )(lg, qkvg, qkvg, qkvg, qkvg, g_ret_out.reshape(1, RET_WIDTH))


def _out_proj_kernel(yc_ref, yr_ref, w_ref, gc_ref, h_ref, gf_ref, o_ref, fu_ref, rs_ref,
                     ycn_ref, ss_ref, *, d_model):
    n = pl.program_id(1)
    kc = yc_ref.shape[1]

    @pl.when(n == 0)
    def _():
        yc = yc_ref[...].astype(F32)
        ms = jnp.mean(yc * yc, axis=-1, keepdims=True)
        ycn_ref[...] = (yc * lax.rsqrt(ms + EPS) * gc_ref[...]).astype(BF16)
        ss_ref[...] = jnp.zeros_like(ss_ref)

    acc = jnp.dot(ycn_ref[...], w_ref[0:kc, :], preferred_element_type=F32)
    acc = acc + jnp.dot(yr_ref[...], w_ref[kc:, :], preferred_element_type=F32)
    hn = h_ref[...] + acc
    o_ref[...] = hn
    fu_ref[...] = (hn * gf_ref[...]).astype(fu_ref.dtype)
    ss_ref[...] += jnp.sum(hn * hn, axis=-1, keepdims=True)

    @pl.when(n == pl.num_programs(1) - 1)
    def _():
        rs = lax.rsqrt(ss_ref[...] / d_model + EPS)
        rs_ref[...] = jnp.broadcast_to(rs, rs_ref.shape)


def out_proj(yc, yr, w_out_bf16, layer, g_conv_out, h, g_ffn):
    s, d = h.shape
    bm, bn = OUT_ROWS, OUT_COLS
    kc, kr = yc.shape[1], yr.shape[1]
    return pl.pallas_call(
        functools.partial(_out_proj_kernel, d_model=d),
        grid=(s // bm, d // bn),
        in_specs=[pl.BlockSpec((bm, kc), lambda m, n: (m, 0)),
                  pl.BlockSpec((bm, kr), lambda m, n: (m, 0)),
                  pl.BlockSpec((None, kc + kr, bn), lambda m, n: (layer, 0, n)),
                  pl.BlockSpec((1, kc), lambda m, n: (0, 0)),
                  pl.BlockSpec((bm, bn), lambda m, n: (m, n)),
                  pl.BlockSpec((1, bn), lambda m, n: (0, n))],
        out_specs=[pl.BlockSpec((bm, bn), lambda m, n: (m, n)),
                   pl.BlockSpec((bm, bn), lambda m, n: (m, n)),
                   pl.BlockSpec((bm, LANES), lambda m, n: (m, 0))],
        out_shape=[jax.ShapeDtypeStruct((s, d), F32),
                   jax.ShapeDtypeStruct((s, d), BF16),
                   jax.ShapeDtypeStruct((s, LANES), F32)],
        scratch_shapes=[pltpu.VMEM((bm, kc), BF16), pltpu.VMEM((bm, 1), F32)],
        compiler_params=_params("arbitrary", "arbitrary"),
        name="out_proj",
    )(yc, yr, w_out_bf16, g_conv_out.reshape(1, kc), h, g_ffn.reshape(1, d))


def _swiglu_kernel(a_ref, rs_ref, wg_ref, wu_ref, o_ref, wbf_ref):
    tf = FFN_COLS

    @pl.when(pl.program_id(1) == 0)
    def _():
        wbf_ref[:, 0:tf] = wg_ref[...].astype(BF16)
        wbf_ref[:, tf:2 * tf] = wu_ref[...].astype(BF16)

    acc = jnp.dot(a_ref[...], wbf_ref[...], preferred_element_type=F32) * rs_ref[:, 0:1]
    o_ref[...] = (_silu(acc[:, 0:tf]) * acc[:, tf:2 * tf]).astype(o_ref.dtype)


def swiglu_up(fu, rs, wg, wu, layer):
    s, d = fu.shape
    n = wg.shape[2]
    bm, tf = FFN_ROWS, FFN_COLS
    return pl.pallas_call(
        _swiglu_kernel,
        grid=(n // tf, s // bm),
        in_specs=[pl.BlockSpec((bm, d), lambda j, m: (m, 0)),
                  pl.BlockSpec((bm, LANES), lambda j, m: (m, 0)),
                  pl.BlockSpec((None, d, tf), lambda j, m: (layer, 0, j)),
                  pl.BlockSpec((None, d, tf), lambda j, m: (layer, 0, j))],
        out_specs=pl.BlockSpec((bm, tf), lambda j, m: (m, j)),
        out_shape=jax.ShapeDtypeStruct((s, n), BF16),
        scratch_shapes=[pltpu.VMEM((d, 2 * tf), BF16)],
        compiler_params=_params("arbitrary", "arbitrary"),
        name="swiglu_up",
    )(fu, rs, wg, wu)


def _down_kernel(a_ref, w_ref, h_ref, o_ref):
    o_ref[...] = h_ref[...] + jnp.dot(a_ref[...], w_ref[...], preferred_element_type=F32)


def ffn_down(hid, w_bf16, layer, h):
    s, k = hid.shape
    d = w_bf16.shape[2]
    bm, bn = DOWN_ROWS, DOWN_COLS
    return pl.pallas_call(
        _down_kernel,
        grid=(s // bm, d // bn),
        in_specs=[pl.BlockSpec((bm, k), lambda i, j: (i, 0)),
                  pl.BlockSpec((None, k, bn), lambda i, j: (layer, 0, j)),
                  pl.BlockSpec((bm, bn), lambda i, j: (i, j))],
        out_specs=pl.BlockSpec((bm, bn), lambda i, j: (i, j)),
        out_shape=jax.ShapeDtypeStruct((s, d), F32),
        compiler_params=_params("arbitrary", "arbitrary"),
        name="ffn_down",
    )(hid, w_bf16, h)


def _ple_kernel(h_ref, p_ref, gple_ref, wd_ref, wu_ref, wp_ref, gnext_ref, *out_refs, emit_h):
    rows = h_ref.shape[0] // PLE_ROW_GROUPS
    for grp in range(PLE_ROW_GROUPS):
        sl = pl.ds(grp * rows, rows)
        h = h_ref[sl, :]
        ms = jnp.mean(h * h, axis=-1, keepdims=True)
        r = (h * lax.rsqrt(ms + EPS) * gple_ref[...]).astype(BF16)
        low = jnp.dot(r, wd_ref[...], preferred_element_type=F32)
        gate = jax.nn.sigmoid(jnp.dot(low.astype(BF16), wu_ref[...], preferred_element_type=F32))
        emb = jnp.dot(p_ref[sl, :].astype(BF16), wp_ref[...], preferred_element_type=F32)
        hn = h + gate * emb
        ms2 = jnp.mean(hn * hn, axis=-1, keepdims=True)
        normed = hn * lax.rsqrt(ms2 + EPS) * gnext_ref[...]
        if emit_h:
            out_refs[0][sl, :] = hn
            out_refs[1][sl, :] = normed.astype(out_refs[1].dtype)
        else:
            out_refs[0][sl, :] = normed.astype(out_refs[0].dtype)


def ple_and_norm(h, p, layer, g_ple, wd, wu, wp, g_next, last):
    s, d = h.shape
    bm = NORM_ROWS
    pdim = p.shape[-1]
    rank = wd.shape[1]
    row = pl.BlockSpec((bm, d), lambda i: (i, 0))
    vec = pl.BlockSpec((1, d), lambda i: (0, 0))
    if last:
        out_specs = row
        out_shape = jax.ShapeDtypeStruct((s, d), F32)
    else:
        out_specs = [row, row]
        out_shape = [jax.ShapeDtypeStruct((s, d), F32), jax.ShapeDtypeStruct((s, d), BF16)]
    return pl.pallas_call(
        functools.partial(_ple_kernel, emit_h=not last),
        grid=(s // bm,),
        in_specs=[row,
                  pl.BlockSpec((None, None, bm, pdim), lambda i: (layer, 0, i, 0)),
                  vec,
                  pl.BlockSpec((d, rank), lambda i: (0, 0)),
                  pl.BlockSpec((rank, d), lambda i: (0, 0)),
                  pl.BlockSpec((pdim, d), lambda i: (0, 0)),
                  vec],
        out_specs=out_specs,
        out_shape=out_shape,
        compiler_params=_params("arbitrary"),
        name="ple",
    )(h, p, g_ple.reshape(1, d), wd, wu, wp, g_next.reshape(1, d))


def kernel(x, p, g_mix, w_in, conv_w, conv_b, g_conv_out, g_ret_out, w_out, g_ffn,
           w_ffn_gate, w_ffn_up, w_ffn_down, g_ple, w_ple_gate_down, w_ple_gate_up,
           w_ple_proj, g_final):
    b, s, d = x.shape
    assert b == 1
    depth = w_in.shape[0]
    h = x.reshape(s, d)

    pos = jnp.arange(s, dtype=F32)
    inv_freq = ROPE_BASE ** (-jnp.arange(HALF, dtype=F32) / HALF)
    ang = pos[:, None] * inv_freq[None, :]
    cos, sin = jnp.cos(ang), jnp.sin(ang)
    log_gamma = jnp.log1p(-jnp.exp2(-5.0 - jnp.arange(RET_HEADS, dtype=F32)))

    w_out_bf16 = round_to_bf16(w_out)
    w_down_bf16 = round_to_bf16(w_ffn_down)

    a = rmsnorm(h, g_mix[0], BF16)
    for i in range(depth):
        yc = conv_proj(a, w_in, i, conv_w[i], conv_b[i])
        qkvg = head_proj(a, w_in, i, cos, sin)
        yr = retention(qkvg, g_ret_out[i], log_gamma)
        h, fu, rs = out_proj(yc, yr, w_out_bf16, i, g_conv_out[i], h, g_ffn[i])
        hid = swiglu_up(fu, rs, w_ffn_gate, w_ffn_up, i)
        h = ffn_down(hid, w_down_bf16, i, h)

        last = i == depth - 1
        g_next = g_final if last else g_mix[i + 1]
        res = ple_and_norm(h, p, i, g_ple[i], w_ple_gate_down[i].astype(BF16),
                           w_ple_gate_up[i].astype(BF16), w_ple_proj[i].astype(BF16),
                           g_next, last)
        if last:
            return res.reshape(b, s, d)
        h, a = res
```

```python
import functools

import jax
import jax.numpy as jnp
from jax import lax
from jax.experimental import pallas as pl
from jax.experimental.pallas import tpu as pltpu

EPS = 1e-6
CHUNK = 64
CONV_WIDTH = 2048
RET_WIDTH = 2048
RET_HEADS = 8
HEAD_DIM = 256
HALF = HEAD_DIM // 2
ROPE_BASE = 10000.0

F32 = jnp.float32
BF16 = jnp.bfloat16

SUBLANES = 8
LANES = 128
V7X_VMEM_LIMIT = 62 * 1024 * 1024

CONV_ROWS = 1024
CONV_COLS = 256
HEAD_ROWS = 1024
HEAD_COLS = 1024
OUT_ROWS = 1024
OUT_COLS = 512
CAST_ROWS = 512
FFN_ROWS = 2048
FFN_COLS = 256
DOWN_ROWS = 512
DOWN_COLS = 512
RET_ROWS = 2048
RET_BLOCK = 256
NORM_ROWS = 256
PLE_ROW_GROUPS = 2


def _params(*sem):
    return pltpu.CompilerParams(dimension_semantics=sem, vmem_limit_bytes=V7X_VMEM_LIMIT)


def _silu(x):
    return x * jax.nn.sigmoid(x)


def _rmsnorm_kernel(x_ref, g_ref, o_ref):
    x = x_ref[...]
    ms = jnp.mean(x * x, axis=-1, keepdims=True)
    o_ref[...] = (x * lax.rsqrt(ms + EPS) * g_ref[...]).astype(o_ref.dtype)


def rmsnorm(x, g, out_dtype):
    s, d = x.shape
    bm = NORM_ROWS
    return pl.pallas_call(
        _rmsnorm_kernel,
        grid=(s // bm,),
        in_specs=[pl.BlockSpec((bm, d), lambda i: (i, 0)),
                  pl.BlockSpec((1, d), lambda i: (0, 0))],
        out_specs=pl.BlockSpec((bm, d), lambda i: (i, 0)),
        out_shape=jax.ShapeDtypeStruct((s, d), out_dtype),
        compiler_params=_params("arbitrary"),
        name="rmsnorm",
    )(x, g.reshape(1, d))


def _round_kernel(x_ref, o_ref):
    o_ref[...] = x_ref[...].astype(o_ref.dtype)


def round_to_bf16(w):
    depth, k, n = w.shape
    rows = depth * k
    bm = CAST_ROWS
    assert rows % bm == 0
    out = pl.pallas_call(
        _round_kernel,
        grid=(rows // bm,),
        in_specs=[pl.BlockSpec((bm, n), lambda i: (i, 0))],
        out_specs=pl.BlockSpec((bm, n), lambda i: (i, 0)),
        out_shape=jax.ShapeDtypeStruct((rows, n), BF16),
        compiler_params=_params("arbitrary"),
        name="round_to_bf16",
    )(w.reshape(rows, n))
    return out.reshape(depth, k, n)


def _prefetched_weight_tile(w_hbm, windows, stage_ref, wbf_ref, sems):
    n = pl.program_id(0)

    def copies(j):
        return [pltpu.make_async_copy(src, dst, sems.at[i])
                for i, (src, dst) in enumerate(windows(j))]

    @pl.when(n == 0)
    def _():
        for c in copies(0):
            c.start()

    for c in copies(n):
        c.wait()
    wbf_ref[...] = stage_ref[...].astype(BF16)

    @pl.when(n + 1 < pl.num_programs(0))
    def _():
        for c in copies(n + 1):
            c.start()


def _conv_proj_kernel(a_ref, w_hbm, cw_ref, cbias_ref, o_ref, stage_ref, wbf_ref, halo_ref, sems,
                      *, layer):
    tc = CONV_COLS

    def windows(j):
        return [(w_hbm.at[layer, :, pl.ds(sec * CONV_WIDTH + j * tc, tc)],
                 stage_ref.at[:, pl.ds(sec * tc, tc)]) for sec in range(3)]

    @pl.when(pl.program_id(1) == 0)
    def _():
        _prefetched_weight_tile(w_hbm, windows, stage_ref, wbf_ref, sems)
        halo_ref[...] = jnp.zeros_like(halo_ref)

    acc = jnp.dot(a_ref[...], wbf_ref[...], preferred_element_type=F32)
    cb = acc[:, 0:tc]
    u = acc[:, tc:2 * tc] * acc[:, 2 * tc:3 * tc]
    bm = u.shape[0]
    row = lax.broadcasted_iota(jnp.int32, (bm, tc), 0)
    halo = halo_ref[...]
    prev1 = halo[SUBLANES - 1:SUBLANES, :]
    prev2 = halo[SUBLANES - 2:SUBLANES - 1, :]
    u1 = jnp.where(row == 0, prev1, pltpu.roll(u, 1, axis=0))
    u2 = jnp.where(row == 0, prev2, jnp.where(row == 1, prev1, pltpu.roll(u, 2, axis=0)))
    halo_ref[...] = u[bm - SUBLANES:, :]
    cw = cw_ref[...]
    conv = u2 * cw[0:1, :] + u1 * cw[1:2, :] + u * cw[2:3, :] + cbias_ref[...]
    o_ref[...] = (cb * conv).astype(o_ref.dtype)


def conv_proj(a, w_in, layer, conv_w, conv_b):
    s, d = a.shape
    bm, tc = CONV_ROWS, CONV_COLS
    return pl.pallas_call(
        functools.partial(_conv_proj_kernel, layer=layer),
        grid=(CONV_WIDTH // tc, s // bm),
        in_specs=[pl.BlockSpec((bm, d), lambda n, m: (m, 0)),
                  pl.BlockSpec(memory_space=pl.ANY),
                  pl.BlockSpec((3, tc), lambda n, m: (0, n)),
                  pl.BlockSpec((1, tc), lambda n, m: (0, n))],
        out_specs=pl.BlockSpec((bm, tc), lambda n, m: (m, n)),
        out_shape=jax.ShapeDtypeStruct((s, CONV_WIDTH), BF16),
        scratch_shapes=[pltpu.VMEM((d, 3 * tc), F32),
                        pltpu.VMEM((d, 3 * tc), BF16),
                        pltpu.VMEM((SUBLANES, tc), F32),
                        pltpu.SemaphoreType.DMA((3,))],
        compiler_params=_params("arbitrary", "arbitrary"),
        name="conv_proj",
    )(a, w_in, conv_w, conv_b.reshape(1, CONV_WIDTH))


def _head_proj_kernel(a_ref, w_hbm, cos_ref, sin_ref, o_ref, stage_ref, wbf_ref, sems,
                      *, layer, blocks_per_section):
    n = pl.program_id(0)
    nb = blocks_per_section
    bn = HEAD_COLS

    def windows(j):
        return [(w_hbm.at[layer, :, pl.ds(3 * CONV_WIDTH + j * bn, bn)], stage_ref)]

    @pl.when(pl.program_id(1) == 0)
    def _():
        _prefetched_weight_tile(w_hbm, windows, stage_ref, wbf_ref, sems)

    def project():
        return jnp.dot(a_ref[...], wbf_ref[...], preferred_element_type=F32)

    @pl.when(n < 2 * nb)
    def _():
        acc = project()
        scale = jnp.where(n < nb, HEAD_DIM ** -0.5, 1.0).astype(F32)
        cos = cos_ref[...]
        sin = sin_ref[...]
        for hd in range(bn // HEAD_DIM):
            c = hd * HEAD_DIM
            x1 = acc[:, c:c + HALF]
            x2 = acc[:, c + HALF:c + HEAD_DIM]
            o_ref[:, c:c + HALF] = ((x1 * cos - x2 * sin) * scale).astype(o_ref.dtype)
            o_ref[:, c + HALF:c + HEAD_DIM] = ((x2 * cos + x1 * sin) * scale).astype(o_ref.dtype)

    @pl.when(jnp.logical_and(n >= 2 * nb, n < 3 * nb))
    def _():
        o_ref[...] = project().astype(o_ref.dtype)

    @pl.when(n >= 3 * nb)
    def _():
        o_ref[...] = _silu(project()).astype(o_ref.dtype)


def head_proj(a, w_in, layer, cos, sin):
    s, d = a.shape
    bm, bn = HEAD_ROWS, HEAD_COLS
    nb = RET_WIDTH // bn
    return pl.pallas_call(
        functools.partial(_head_proj_kernel, layer=layer, blocks_per_section=nb),
        grid=(4 * nb, s // bm),
        in_specs=[pl.BlockSpec((bm, d), lambda n, m: (m, 0)),
                  pl.BlockSpec(memory_space=pl.ANY),
                  pl.BlockSpec((bm, HALF), lambda n, m: (m, 0)),
                  pl.BlockSpec((bm, HALF), lambda n, m: (m, 0))],
        out_specs=pl.BlockSpec((bm, bn), lambda n, m: (m, n)),
        out_shape=jax.ShapeDtypeStruct((s, 4 * RET_WIDTH), BF16),
        scratch_shapes=[pltpu.VMEM((d, bn), F32),
                        pltpu.VMEM((d, bn), BF16),
                        pltpu.SemaphoreType.DMA((1,))],
        compiler_params=_params("arbitrary", "arbitrary"),
        name="head_proj",
    )(a, w_in, cos, sin)


def _retention_kernel(lg_ref, q_ref, k_ref, v_ref, sg_ref, gro_ref, o_ref,
                      state_ref, dmask_ref, qd_ref, kd_ref):
    t = RET_BLOCK
    lg = lg_ref[0]

    @pl.when(pl.program_id(1) == 0)
    def _():
        state_ref[...] = jnp.zeros_like(state_ref)
        i = lax.broadcasted_iota(jnp.int32, (t, t), 0)
        j = lax.broadcasted_iota(jnp.int32, (t, t), 1)
        dist = jnp.abs(i - j).astype(F32)
        visible = (j // CHUNK) <= (i // CHUNK)
        dmask_ref[...] = jnp.where(visible, jnp.exp(dist * lg), 0.0)
        fi = i.astype(F32)
        qd_ref[...] = jnp.exp((fi + 1.0) * lg)
        kd_ref[...] = jnp.exp((t - 1.0 - fi) * lg)

    block_decay = jnp.exp(float(t) * lg)
    gro = gro_ref[...]
    for blk in range(RET_ROWS // t):
        rows = pl.ds(blk * t, t)
        q = q_ref[rows, :]
        k = k_ref[rows, :]
        v = v_ref[rows, :]
        scores = lax.dot_general(q, k, (((1,), (1,)), ((), ())), preferred_element_type=F32)
        scores = scores * dmask_ref[...]
        o = jnp.dot(scores.astype(BF16), v, preferred_element_type=F32)
        state = state_ref[...]
        qs = (q.astype(F32) * qd_ref[...]).astype(BF16)
        o = o + jnp.dot(qs, state.astype(BF16), preferred_element_type=F32)
        ks = (k.astype(F32) * kd_ref[...]).astype(BF16)
        kv = lax.dot_general(ks, v, (((0,), (0,)), ((), ())), preferred_element_type=F32)
        state_ref[...] = state * block_decay + kv
        o = o * lax.rsqrt(jnp.mean(o * o, axis=-1, keepdims=True) + EPS)
        o_ref[rows, :] = ((o * gro) * sg_ref[rows, :].astype(F32)).astype(o_ref.dtype)


def retention(qkvg, g_ret_out, log_gamma):
    s = qkvg.shape[0]
    bm = RET_ROWS
    nh = RET_HEADS

    def col(section):
        return pl.BlockSpec((bm, HEAD_DIM), lambda h, i: (i, section * nh + h))

    lg = jnp.broadcast_to(log_gamma[:, None, None], (nh, 1, HEAD_DIM))
    return pl.pallas_call(
        _retention_kernel,
        grid=(nh, s // bm),
        in_specs=[pl.BlockSpec((1, 1, HEAD_DIM), lambda h, i: (h, 0, 0)),
                  col(0), col(1), col(2), col(3),
                  pl.BlockSpec((1, HEAD_DIM), lambda h, i: (0, h))],
        out_specs=pl.BlockSpec((bm, HEAD_DIM), lambda h, i: (i, h)),
        out_shape=jax.ShapeDtypeStruct((s, RET_WIDTH), BF16),
        scratch_shapes=[pltpu.VMEM((HEAD_DIM, HEAD_DIM), F32),
                        pltpu.VMEM((RET_BLOCK, RET_BLOCK), F32),
                        pltpu.VMEM((RET_BLOCK, HEAD_DIM), F32),
                        pltpu.VMEM((RET_BLOCK, HEAD_DIM), F32)],
        compiler_params=_params("arbitrary", "arbitrary"),
        name="retention",
    )(lg, qkvg, qkvg, qkvg, qkvg, g_ret_out.reshape(1, RET_WIDTH))


def _out_proj_kernel(yc_ref, yr_ref, w_ref, gc_ref, h_ref, gf_ref, o_ref, fu_ref, rs_ref,
                     ycn_ref, ss_ref, *, d_model):
    n = pl.program_id(1)
    kc = yc_ref.shape[1]

    @pl.when(n == 0)
    def _():
        yc = yc_ref[...].astype(F32)
        ms = jnp.mean(yc * yc, axis=-1, keepdims=True)
        ycn_ref[...] = (yc * lax.rsqrt(ms + EPS) * gc_ref[...]).astype(BF16)
        ss_ref[...] = jnp.zeros_like(ss_ref)

    acc = jnp.dot(ycn_ref[...], w_ref[0:kc, :], preferred_element_type=F32)
    acc = acc + jnp.dot(yr_ref[...], w_ref[kc:, :], preferred_element_type=F32)
    hn = h_ref[...] + acc
    o_ref[...] = hn
    fu_ref[...] = (hn * gf_ref[...]).astype(fu_ref.dtype)
    ss_ref[...] += jnp.sum(hn * hn, axis=-1, keepdims=True)

    @pl.when(n == pl.num_programs(1) - 1)
    def _():
        rs = lax.rsqrt(ss_ref[...] / d_model + EPS)
        rs_ref[...] = jnp.broadcast_to(rs, rs_ref.shape)


def out_proj(yc, yr, w_out_bf16, layer, g_conv_out, h, g_ffn):
    s, d = h.shape
    bm, bn = OUT_ROWS, OUT_COLS
    kc, kr = yc.shape[1], yr.shape[1]
    return pl.pallas_call(
        functools.partial(_out_proj_kernel, d_model=d),
        grid=(s // bm, d // bn),
        in_specs=[pl.BlockSpec((bm, kc), lambda m, n: (m, 0)),
                  pl.BlockSpec((bm, kr), lambda m, n: (m, 0)),
                  pl.BlockSpec((None, kc + kr, bn), lambda m, n: (layer, 0, n)),
                  pl.BlockSpec((1, kc), lambda m, n: (0, 0)),
                  pl.BlockSpec((bm, bn), lambda m, n: (m, n)),
                  pl.BlockSpec((1, bn), lambda m, n: (0, n))],
        out_specs=[pl.BlockSpec((bm, bn), lambda m, n: (m, n)),
                   pl.BlockSpec((bm, bn), lambda m, n: (m, n)),
                   pl.BlockSpec((bm, LANES), lambda m, n: (m, 0))],
        out_shape=[jax.ShapeDtypeStruct((s, d), F32),
                   jax.ShapeDtypeStruct((s, d), BF16),
                   jax.ShapeDtypeStruct((s, LANES), F32)],
        scratch_shapes=[pltpu.VMEM((bm, kc), BF16), pltpu.VMEM((bm, 1), F32)],
        compiler_params=_params("arbitrary", "arbitrary"),
        name="out_proj",
    )(yc, yr, w_out_bf16, g_conv_out.reshape(1, kc), h, g_ffn.reshape(1, d))


def _swiglu_kernel(a_ref, rs_ref, wg_ref, wu_ref, o_ref, wbf_ref):
    tf = FFN_COLS

    @pl.when(pl.program_id(1) == 0)
    def _():
        wbf_ref[:, 0:tf] = wg_ref[...].astype(BF16)
        wbf_ref[:, tf:2 * tf] = wu_ref[...].astype(BF16)

    acc = jnp.dot(a_ref[...], wbf_ref[...], preferred_element_type=F32) * rs_ref[:, 0:1]
    o_ref[...] = (_silu(acc[:, 0:tf]) * acc[:, tf:2 * tf]).astype(o_ref.dtype)


def swiglu_up(fu, rs, wg, wu, layer):
    s, d = fu.shape
    n = wg.shape[2]
    bm, tf = FFN_ROWS, FFN_COLS
    return pl.pallas_call(
        _swiglu_kernel,
        grid=(n // tf, s // bm),
        in_specs=[pl.BlockSpec((bm, d), lambda j, m: (m, 0)),
                  pl.BlockSpec((bm, LANES), lambda j, m: (m, 0)),
                  pl.BlockSpec((None, d, tf), lambda j, m: (layer, 0, j)),
                  pl.BlockSpec((None, d, tf), lambda j, m: (layer, 0, j))],
        out_specs=pl.BlockSpec((bm, tf), lambda j, m: (m, j)),
        out_shape=jax.ShapeDtypeStruct((s, n), BF16),
        scratch_shapes=[pltpu.VMEM((d, 2 * tf), BF16)],
        compiler_params=_params("arbitrary", "arbitrary"),
        name="swiglu_up",
    )(fu, rs, wg, wu)


def _down_kernel(hid_hbm, w_ref, h_ref, o_ref, hid_buf, sems):
    m = pl.program_id(0)
    bm = hid_buf.shape[1]
    slot = jnp.bitwise_and(m, 1)

    def hid_copy(i, s):
        return pltpu.make_async_copy(hid_hbm.at[pl.ds(i * bm, bm), :], hid_buf.at[s], sems.at[s])

    @pl.when(pl.program_id(1) == 0)
    def _():
        @pl.when(m == 0)
        def _():
            hid_copy(0, 0).start()

        hid_copy(m, slot).wait()

        @pl.when(m + 1 < pl.num_programs(0))
        def _():
            hid_copy(m + 1, 1 - slot).start()

    o_ref[...] = h_ref[...] + jnp.dot(hid_buf[slot], w_ref[...], preferred_element_type=F32)


def ffn_down(hid, w_bf16, layer, h):
    s, k = hid.shape
    d = w_bf16.shape[2]
    bm, bn = DOWN_ROWS, DOWN_COLS
    return pl.pallas_call(
        _down_kernel,
        grid=(s // bm, d // bn),
        in_specs=[pl.BlockSpec(memory_space=pl.ANY),
                  pl.BlockSpec((None, k, bn), lambda i, j: (layer, 0, j)),
                  pl.BlockSpec((bm, bn), lambda i, j: (i, j))],
        out_specs=pl.BlockSpec((bm, bn), lambda i, j: (i, j)),
        out_shape=jax.ShapeDtypeStruct((s, d), F32),
        scratch_shapes=[pltpu.VMEM((2, bm, k), BF16), pltpu.SemaphoreType.DMA((2,))],
        compiler_params=_params("arbitrary", "arbitrary"),
        name="ffn_down",
    )(hid, w_bf16, h)


def _ple_kernel(h_ref, p_ref, gple_ref, wd_ref, wu_ref, wp_ref, gnext_ref, *out_refs, emit_h):
    rows = h_ref.shape[0] // PLE_ROW_GROUPS
    for grp in range(PLE_ROW_GROUPS):
        sl = pl.ds(grp * rows, rows)
        h = h_ref[sl, :]
        ms = jnp.mean(h * h, axis=-1, keepdims=True)
        r = (h * lax.rsqrt(ms + EPS) * gple_ref[...]).astype(BF16)
        low = jnp.dot(r, wd_ref[...], preferred_element_type=F32)
        gate = jax.nn.sigmoid(jnp.dot(low.astype(BF16), wu_ref[...], preferred_element_type=F32))
        emb = jnp.dot(p_ref[sl, :].astype(BF16), wp_ref[...], preferred_element_type=F32)
        hn = h + gate * emb
        ms2 = jnp.mean(hn * hn, axis=-1, keepdims=True)
        normed = hn * lax.rsqrt(ms2 + EPS) * gnext_ref[...]
        if emit_h:
            out_refs[0][sl, :] = hn
            out_refs[1][sl, :] = normed.astype(out_refs[1].dtype)
        else:
            out_refs[0][sl, :] = normed.astype(out_refs[0].dtype)


def ple_and_norm(h, p, layer, g_ple, wd, wu, wp, g_next, last):
    s, d = h.shape
    bm = NORM_ROWS
    pdim = p.shape[-1]
    rank = wd.shape[1]
    row = pl.BlockSpec((bm, d), lambda i: (i, 0))
    vec = pl.BlockSpec((1, d), lambda i: (0, 0))
    if last:
        out_specs = row
        out_shape = jax.ShapeDtypeStruct((s, d), F32)
    else:
        out_specs = [row, row]
        out_shape = [jax.ShapeDtypeStruct((s, d), F32), jax.ShapeDtypeStruct((s, d), BF16)]
    return pl.pallas_call(
        functools.partial(_ple_kernel, emit_h=not last),
        grid=(s // bm,),
        in_specs=[row,
                  pl.BlockSpec((None, None, bm, pdim), lambda i: (layer, 0, i, 0)),
                  vec,
                  pl.BlockSpec((d, rank), lambda i: (0, 0)),
                  pl.BlockSpec((rank, d), lambda i: (0, 0)),
                  pl.BlockSpec((pdim, d), lambda i: (0, 0)),
                  vec],
        out_specs=out_specs,
        out_shape=out_shape,
        compiler_params=_params("arbitrary"),
        name="ple",
    )(h, p, g_ple.reshape(1, d), wd, wu, wp, g_next.reshape(1, d))


def kernel(x, p, g_mix, w_in, conv_w, conv_b, g_conv_out, g_ret_out, w_out, g_ffn,
           w_ffn_gate, w_ffn_up, w_ffn_down, g_ple, w_ple_gate_down, w_ple_gate_up,
           w_ple_proj, g_final):
    b, s, d = x.shape
    assert b == 1
    depth = w_in.shape[0]
    h = x.reshape(s, d)

    pos = jnp.arange(s, dtype=F32)
    inv_freq = ROPE_BASE ** (-jnp.arange(HALF, dtype=F32) / HALF)
    ang = pos[:, None] * inv_freq[None, :]
    cos, sin = jnp.cos(ang), jnp.sin(ang)
    log_gamma = jnp.log1p(-jnp.exp2(-5.0 - jnp.arange(RET_HEADS, dtype=F32)))

    w_out_bf16 = round_to_bf16(w_out)
    w_down_bf16 = round_to_bf16(w_ffn_down)

    a = rmsnorm(h, g_mix[0], BF16)
    for i in range(depth):
        yc = conv_proj(a, w_in, i, conv_w[i], conv_b[i])
        qkvg = head_proj(a, w_in, i, cos, sin)
        yr = retention(qkvg, g_ret_out[i], log_gamma)
        h, fu, rs = out_proj(yc, yr, w_out_bf16, i, g_conv_out[i], h, g_ffn[i])
        hid = swiglu_up(fu, rs, w_ffn_gate, w_ffn_up, i)
        h = ffn_down(hid, w_down_bf16, i, h)

        last = i == depth - 1
        g_next = g_final if last else g_mix[i + 1]
        res = ple_and_norm(h, p, i, g_ple[i], w_ple_gate_down[i].astype(BF16),
                           w_ple_gate_up[i].astype(BF16), w_ple_proj[i].astype(BF16),
                           g_next, last)
        if last:
            return res.reshape(b, s, d)
        h, a = res
```
